```python
import jax, jax.numpy as jnp
from jax import lax
import numpy as np

D_MODEL = 4096
BATCH = 4
SEQ = 4096
DEPTH = 2

HEAD_DIM = 128
N_Q_HEADS = 16
N_KV_HEADS = 4
ATTN_WIDTH = N_Q_HEADS * HEAD_DIM
KV_WIDTH = N_KV_HEADS * HEAD_DIM
CONV_WIDTH = D_MODEL // 2
CONV_GROUPS = 16
CONV_K = 3
D_FF = 4 * D_MODEL
GRID_W = 64
Q_BLOCK = 128
ROPE_THETA = 10000.0
NORM_EPS = 1e-6
N_MOD = 6
IN_SPLITS = (ATTN_WIDTH, KV_WIDTH, KV_WIDTH, CONV_WIDTH, CONV_WIDTH, CONV_WIDTH, D_MODEL, D_MODEL)
IN_WIDTH = sum(IN_SPLITS)
MOD_INIT = 0.5

kernel_name = 'hybrid_gqa_shortconv_adaln_encoder'


def rms_norm(x, g):
    xf = x.astype(jnp.float32)
    y = xf * lax.rsqrt(jnp.mean(xf * xf, axis=-1, keepdims=True) + NORM_EPS)
    return (y * g.astype(jnp.float32)).astype(x.dtype)


def axial_rope_tables(seq):
    rows = seq // GRID_W
    row = jnp.repeat(jnp.arange(rows, dtype=jnp.float32), GRID_W)
    col = jnp.tile(jnp.arange(GRID_W, dtype=jnp.float32), rows)
    half = HEAD_DIM // 2
    freqs = ROPE_THETA ** (-jnp.arange(0, half, 2, dtype=jnp.float32) / half)
    ang = jnp.stack([row[:, None] * freqs, col[:, None] * freqs], axis=1)
    return jnp.cos(ang), jnp.sin(ang)


def apply_axial_rope(x, cos, sin):
    b, s, h, _ = x.shape
    xr = x.astype(jnp.float32).reshape(b, s, h, 2, 2, HEAD_DIM // 4)
    x1, x2 = xr[..., 0, :], xr[..., 1, :]
    cs, sn = cos[:, None], sin[:, None]
    out = jnp.stack([x1 * cs - x2 * sn, x1 * sn + x2 * cs], axis=-2)
    return out.reshape(b, s, h, HEAD_DIM).astype(x.dtype)


def block_attention(q, k, v):
    b, s = q.shape[:2]
    nb = s // Q_BLOCK
    grp = N_Q_HEADS // N_KV_HEADS
    qb = q.reshape(b, nb, Q_BLOCK, N_KV_HEADS, grp, HEAD_DIM).transpose(1, 0, 3, 4, 2, 5)
    kt = k.transpose(0, 2, 1, 3)
    vt = v.transpose(0, 2, 1, 3)
    scale = HEAD_DIM ** -0.5

    def one_block(qblk):
        sc = jnp.einsum('bkgqd,bksd->bkgqs', qblk, kt).astype(jnp.float32) * scale
        p = jax.nn.softmax(sc, axis=-1).astype(vt.dtype)
        return jnp.einsum('bkgqs,bksd->bkgqd', p, vt)

    o = lax.map(one_block, qb)
    return o.transpose(1, 0, 4, 2, 3, 5).reshape(b, s, ATTN_WIDTH)


def centred_depthwise_conv(u, w):
    kern = w[:, None, :].astype(u.dtype)
    return lax.conv_general_dilated(
        u, kern, window_strides=(1,), padding=[(CONV_K // 2, CONV_K // 2)],
        dimension_numbers=('NWC', 'WIO', 'NWC'), feature_group_count=u.shape[-1])


def hybrid_mixer(h, w_in, q_g, k_g, conv_w, w_branch_attn, w_branch_conv, w_out):
    b, s, _ = h.shape
    proj = h @ w_in
    cuts = [int(i) for i in np.cumsum(IN_SPLITS)[:-1]]
    q, k, v, u, gate_b, gate_c, g_attn, g_conv = jnp.split(proj, cuts, axis=-1)
    q = rms_norm(q.reshape(b, s, N_Q_HEADS, HEAD_DIM), q_g)
    k = rms_norm(k.reshape(b, s, N_KV_HEADS, HEAD_DIM), k_g)
    v = v.reshape(b, s, N_KV_HEADS, HEAD_DIM)
    cos, sin = axial_rope_tables(s)
    q = apply_axial_rope(q, cos, sin)
    k = apply_axial_rope(k, cos, sin)
    y_attn = block_attention(q, k, v) @ w_branch_attn
    y_conv = (gate_b * centred_depthwise_conv(gate_c * u, conv_w)) @ w_branch_conv
    merged = jax.nn.sigmoid(g_attn) * y_attn + jax.nn.sigmoid(g_conv) * y_conv
    return merged @ w_out


def squared_relu_mlp(h, w1, w2):
    return jnp.square(jax.nn.relu(h @ w1)) @ w2


def setup_inputs(seed: int = 0) -> dict:
    key = jax.random.key(seed)
    ks = jax.random.split(key, 17)
    f32 = jnp.float32
    nrm = lambda k, shp: jax.random.normal(k, shp, dtype=f32)
    return {
        'x': nrm(ks[0], (BATCH, SEQ, D_MODEL)),
        'c': nrm(ks[1], (BATCH, D_MODEL)),
        'mod_w': nrm(ks[2], (DEPTH, D_MODEL, N_MOD * D_MODEL)) * (MOD_INIT * D_MODEL ** -0.5),
        'mod_b': nrm(ks[3], (DEPTH, N_MOD * D_MODEL)) * 0.01,
        'norm1_g': 1.0 + 0.05 * nrm(ks[4], (DEPTH, D_MODEL)),
        'norm2_g': 1.0 + 0.05 * nrm(ks[5], (DEPTH, D_MODEL)),
        'w_in': nrm(ks[6], (DEPTH, D_MODEL, IN_WIDTH)) * D_MODEL ** -0.5,
        'q_norm_g': 1.0 + 0.05 * nrm(ks[7], (DEPTH, HEAD_DIM)),
        'k_norm_g': 1.0 + 0.05 * nrm(ks[8], (DEPTH, HEAD_DIM)),
        'conv_w': nrm(ks[9], (DEPTH, CONV_K, CONV_WIDTH)) * CONV_K ** -0.5,
        'w_branch_attn': nrm(ks[10], (DEPTH, ATTN_WIDTH, D_MODEL)) * ATTN_WIDTH ** -0.5,
        'w_branch_conv': nrm(ks[11], (DEPTH, CONV_WIDTH, D_MODEL)) * CONV_WIDTH ** -0.5,
        'w_out': nrm(ks[12], (DEPTH, D_MODEL, D_MODEL)) * D_MODEL ** -0.5,
        'w_mlp_in': nrm(ks[13], (DEPTH, D_MODEL, D_FF)) * D_MODEL ** -0.5,
        'w_mlp_out': nrm(ks[14], (DEPTH, D_FF, D_MODEL)) * D_FF ** -0.5,
        'final_norm_g': 1.0 + 0.05 * nrm(ks[15], (D_MODEL,)),
    }


def reference(x, c, mod_w, mod_b, norm1_g, norm2_g, w_in, q_norm_g, k_norm_g, conv_w,
              w_branch_attn, w_branch_conv, w_out, w_mlp_in, w_mlp_out, final_norm_g):
    sc_in = jax.nn.silu(c)
    for l in range(DEPTH):
        mod = (sc_in @ mod_w[l] + mod_b[l])[:, None, :]
        sh1, sc1, g1, sh2, sc2, g2 = jnp.split(mod, N_MOD, axis=-1)
        h = rms_norm(x, norm1_g[l]) * (1.0 + sc1) + sh1
        x = x + (1.0 + g1) * hybrid_mixer(h, w_in[l], q_norm_g[l], k_norm_g[l], conv_w[l],
                                          w_branch_attn[l], w_branch_conv[l], w_out[l])
        h = rms_norm(x, norm2_g[l]) * (1.0 + sc2) + sh2
        x = x + (1.0 + g2) * squared_relu_mlp(h, w_mlp_in[l], w_mlp_out[l])
    return rms_norm(x, final_norm_g)
```

```python
import functools

import jax
import jax.numpy as jnp
from jax import lax
from jax.experimental import pallas as pl
from jax.experimental.pallas import tpu as pltpu

F32 = jnp.float32
BF16 = jnp.bfloat16

D_MODEL = 4096
SEQ = 4096
HEAD_DIM = 128
N_Q_HEADS = 16
N_KV_HEADS = 4
Q_GROUP = N_Q_HEADS // N_KV_HEADS
ATTN_WIDTH = N_Q_HEADS * HEAD_DIM
KV_WIDTH = N_KV_HEADS * HEAD_DIM
CONV_WIDTH = D_MODEL // 2
D_FF = 4 * D_MODEL
GRID_W = 64
ROPE_THETA = 10000.0
NORM_EPS = 1e-6
N_MOD = 6
IN_WIDTH = ATTN_WIDTH + 2 * KV_WIDTH + 3 * CONV_WIDTH + 2 * D_MODEL

COL_Q = 0
COL_U = ATTN_WIDTH
COL_GB = COL_U + CONV_WIDTH
COL_GC = COL_GB + CONV_WIDTH
COL_GA = COL_GC + CONV_WIDTH
COL_GV = COL_GA + D_MODEL
COL_K = COL_GV + D_MODEL
COL_V = COL_K + KV_WIDTH

V7X_VMEM_LIMIT_BYTES = 58 * 1024 * 1024
MOD_ROWS = 16


def _params(*sem):
    return pltpu.CompilerParams(dimension_semantics=sem, vmem_limit_bytes=V7X_VMEM_LIMIT_BYTES)


def _mod_kernel(c_ref, w_ref, b_ref, o_ref):
    c = c_ref[...]
    a = (c * jax.nn.sigmoid(c)).astype(BF16)
    w = w_ref[0].astype(BF16)
    o_ref[0] = jnp.dot(a, w, preferred_element_type=F32) + b_ref[0]


def _modulation(c_pad, mod_w, mod_b):
    depth, d, n = mod_w.shape
    tn = 512
    return pl.pallas_call(
        _mod_kernel,
        grid=(depth, n // tn),
        in_specs=[
            pl.BlockSpec((MOD_ROWS, d), lambda l, j: (0, 0)),
            pl.BlockSpec((1, d, tn), lambda l, j: (l, 0, j)),
            pl.BlockSpec((1, 1, tn), lambda l, j: (l, 0, j)),
        ],
        out_specs=pl.BlockSpec((1, MOD_ROWS, tn), lambda l, j: (l, 0, j)),
        out_shape=jax.ShapeDtypeStruct((depth, MOD_ROWS, n), F32),
        compiler_params=_params("arbitrary", "arbitrary"),
        name="modulation",
    )(c_pad, mod_w, mod_b.reshape(depth, 1, n))


def _rms(x):
    return x * lax.rsqrt(jnp.mean(x * x, axis=-1, keepdims=True) + NORM_EPS)


def _norm_mod_kernel(x_ref, g_ref, m_ref, o_ref, *, sh_row, sc_row):
    y = _rms(x_ref[...]) * g_ref[...]
    scale = 1.0 + m_ref[0, sc_row:sc_row + 1, :]
    shift = m_ref[0, sh_row:sh_row + 1, :]
    o_ref[...] = (y * scale + shift).astype(BF16)


def _norm_mod(x, g, mod_l, sh_row, sc_row):
    m, d = x.shape
    tr = 256
    return pl.pallas_call(
        functools.partial(_norm_mod_kernel, sh_row=sh_row, sc_row=sc_row),
        grid=(m // tr,),
        in_specs=[
            pl.BlockSpec((tr, d), lambda i: (i, 0)),
            pl.BlockSpec((1, d), lambda i: (0, 0)),
            pl.BlockSpec((1, N_MOD, d), lambda i: (i * tr // SEQ, 0, 0)),
        ],
        out_specs=pl.BlockSpec((tr, d), lambda i: (i, 0)),
        out_shape=jax.ShapeDtypeStruct((m, d), BF16),
        compiler_params=_params("arbitrary"),
        name="norm_mod",
    )(x, g.reshape(1, d), mod_l)


def _final_norm_kernel(x_ref, g_ref, o_ref):
    o_ref[...] = _rms(x_ref[...]) * g_ref[...]


def _final_norm(x, g):
    m, d = x.shape
    tr = 256
    return pl.pallas_call(
        _final_norm_kernel,
        grid=(m // tr,),
        in_specs=[pl.BlockSpec((tr, d), lambda i: (i, 0)), pl.BlockSpec((1, d), lambda i: (0, 0))],
        out_specs=pl.BlockSpec((tr, d), lambda i: (i, 0)),
        out_shape=jax.ShapeDtypeStruct((m, d), F32),
        compiler_params=_params("arbitrary"),
        name="final_norm",
    )(x, g.reshape(1, d))


def _head_norm_rope(x, g, cos, sin_lo, sin_hi):
    y = _rms(x) * g
    return (y * cos + pltpu.roll(y, HEAD_DIM - 32, axis=1) * sin_lo
            + pltpu.roll(y, 32, axis=1) * sin_hi)


def _inproj_kernel(h_ref, w_ref, qg_ref, kg_ref, cos_ref, slo_ref, shi_ref, o_ref, *, tn):
    j = pl.program_id(1)
    acc = jnp.dot(h_ref[...], w_ref[...], preferred_element_type=F32)
    heads_per_tile = tn // HEAD_DIM
    kv_tile = COL_K // tn

    @pl.when(j < COL_U // tn)
    def _():
        cos, slo, shi = cos_ref[...], slo_ref[...], shi_ref[...]
        g = qg_ref[...] * (HEAD_DIM ** -0.5)
        for h in range(heads_per_tile):
            sl = slice(h * HEAD_DIM, (h + 1) * HEAD_DIM)
            o_ref[:, sl] = _head_norm_rope(acc[:, sl], g, cos, slo, shi).astype(BF16)

    @pl.when(j == kv_tile)
    def _():
        cos, slo, shi = cos_ref[...], slo_ref[...], shi_ref[...]
        g = kg_ref[...]
        for h in range(N_KV_HEADS):
            sl = slice(h * HEAD_DIM, (h + 1) * HEAD_DIM)
            o_ref[:, sl] = _head_norm_rope(acc[:, sl], g, cos, slo, shi).astype(BF16)
        o_ref[:, KV_WIDTH:] = acc[:, KV_WIDTH:].astype(BF16)

    @pl.when((j >= COL_U // tn) & (j < kv_tile))
    def _():
        o_ref[...] = acc.astype(BF16)


def _inproj(h, w, qg, kg, cos, slo, shi):
    m, d = h.shape
    n = w.shape[1]
    tm, tn = 1024, 1024
    assert COL_K % tn == 0 and COL_U % tn == 0 and n - COL_K == tn and SEQ % tm == 0
    tab = pl.BlockSpec((tm, HEAD_DIM), lambda i, j: (i % (SEQ // tm), 0))
    vec = pl.BlockSpec((1, HEAD_DIM), lambda i, j: (0, 0))
    return pl.pallas_call(
        functools.partial(_inproj_kernel, tn=tn),
        grid=(m // tm, n // tn),
        in_specs=[
            pl.BlockSpec((tm, d), lambda i, j: (i, 0)),
            pl.BlockSpec((d, tn), lambda i, j: (0, j)),
            vec, vec, tab, tab, tab,
        ],
        out_specs=pl.BlockSpec((tm, tn), lambda i, j: (i, j)),
        out_shape=jax.ShapeDtypeStruct((m, n), BF16),
        compiler_params=_params("arbitrary", "arbitrary"),
        name="inproj",
    )(h, w, qg.reshape(1, HEAD_DIM), kg.reshape(1, HEAD_DIM), cos, slo, shi)


def _attn_kernel(q_ref, k_ref, v_ref, o_ref):
    k = k_ref[...]
    v = v_ref[...]
    for g in range(Q_GROUP):
        sl = slice(g * HEAD_DIM, (g + 1) * HEAD_DIM)
        s = lax.dot_general(q_ref[:, sl], k, (((1,), (1,)), ((), ())), preferred_element_type=F32)
        p = jnp.exp(s - jnp.max(s, axis=-1, keepdims=True))
        denom = jnp.sum(p, axis=-1, keepdims=True)
        o = jnp.dot(p.astype(BF16), v, preferred_element_type=F32)
        o_ref[:, sl] = (o / denom).astype(BF16)


def _attention(proj, batch):
    m = proj.shape[0]
    tq = 256
    nq = SEQ // tq
    gw = Q_GROUP * HEAD_DIM
    return pl.pallas_call(
        _attn_kernel,
        grid=(batch, N_KV_HEADS, nq),
        in_specs=[
            pl.BlockSpec((tq, gw), lambda b, h, i: (b * nq + i, h)),
            pl.BlockSpec((SEQ, HEAD_DIM), lambda b, h, i: (b, COL_K // HEAD_DIM + h)),
            pl.BlockSpec((SEQ, HEAD_DIM), lambda b, h, i: (b, COL_V // HEAD_DIM + h)),
        ],
        out_specs=pl.BlockSpec((tq, gw), lambda b, h, i: (b * nq + i, h)),
        out_shape=jax.ShapeDtypeStruct((m, ATTN_WIDTH), BF16),
        compiler_params=_params("arbitrary", "arbitrary", "arbitrary"),
        name="attention",
    )(proj, proj, proj)


def _conv_kernel(u_ref, gb_ref, gc_ref, up_ref, gcp_ref, un_ref, gcn_ref, w_ref, o_ref, *, tr):
    i = pl.program_id(0)
    cu = gc_ref[...].astype(F32) * u_ref[...].astype(F32)
    starts_seq = (i * tr) % SEQ == 0
    ends_seq = ((i + 1) * tr) % SEQ == 0
    prev_row = gcp_ref[7:8, :].astype(F32) * up_ref[7:8, :].astype(F32)
    next_row = gcn_ref[0:1, :].astype(F32) * un_ref[0:1, :].astype(F32)
    prev_row = jnp.where(starts_seq, 0.0, prev_row)
    next_row = jnp.where(ends_seq, 0.0, next_row)
    row = lax.broadcasted_iota(jnp.int32, cu.shape, 0)
    cu_prev = jnp.where(row == 0, prev_row, pltpu.roll(cu, 1, axis=0))
    cu_next = jnp.where(row == tr - 1, next_row, pltpu.roll(cu, tr - 1, axis=0))
    y = w_ref[0:1, :] * cu_prev + w_ref[1:2, :] * cu + w_ref[2:3, :] * cu_next
    o_ref[...] = (gb_ref[...].astype(F32) * y).astype(BF16)


def _gated_conv(proj, conv_w):
    m = proj.shape[0]
    tr, tc = 512, 512
    nrb = m // 8
    cu, cgb, cgc = COL_U // tc, COL_GB // tc, COL_GC // tc
    main = lambda off: pl.BlockSpec((tr, tc), lambda i, j: (i, off + j))
    prev = lambda off: pl.BlockSpec((8, tc), lambda i, j: (jnp.maximum(i * (tr // 8) - 1, 0), off + j))
    nxt = lambda off: pl.BlockSpec((8, tc), lambda i, j: (jnp.minimum((i + 1) * (tr // 8), nrb - 1), off + j))
    return pl.pallas_call(
        functools.partial(_conv_kernel, tr=tr),
        grid=(m // tr, CONV_WIDTH // tc),
        in_specs=[main(cu), main(cgb), main(cgc), prev(cu), prev(cgc), nxt(cu), nxt(cgc),
                  pl.BlockSpec((3, tc), lambda i, j: (0, j))],
        out_specs=pl.BlockSpec((tr, tc), lambda i, j: (i, j)),
        out_shape=jax.ShapeDtypeStruct((m, CONV_WIDTH), BF16),
        compiler_params=_params("arbitrary", "arbitrary"),
        name="gated_conv",
    )(proj, proj, proj, proj, proj, proj, proj, conv_w)


def _merge_kernel(a_ref, c_ref, wa_ref, wc_ref, ga_ref, gv_ref, o_ref):
    ya = jnp.dot(a_ref[...], wa_ref[...], preferred_element_type=F32)
    yc = jnp.dot(c_ref[...], wc_ref[...], preferred_element_type=F32)
    ga = jax.nn.sigmoid(ga_ref[...].astype(F32))
    gv = jax.nn.sigmoid(gv_ref[...].astype(F32))
    o_ref[...] = (ga * ya + gv * yc).astype(BF16)


def _merge(attn, conv, wa, wc, proj):
    m = attn.shape[0]
    tm, tn = 1024, 512
    return pl.pallas_call(
        _merge_kernel,
        grid=(m // tm, D_MODEL // tn),
        in_specs=[
            pl.BlockSpec((tm, ATTN_WIDTH), lambda i, j: (i, 0)),
            pl.BlockSpec((tm, CONV_WIDTH), lambda i, j: (i, 0)),
            pl.BlockSpec((ATTN_WIDTH, tn), lambda i, j: (0, j)),
            pl.BlockSpec((CONV_WIDTH, tn), lambda i, j: (0, j)),
            pl.BlockSpec((tm, tn), lambda i, j: (i, COL_GA // tn + j)),
            pl.BlockSpec((tm, tn), lambda i, j: (i, COL_GV // tn + j)),
        ],
        out_specs=pl.BlockSpec((tm, tn), lambda i, j: (i, j)),
        out_shape=jax.ShapeDtypeStruct((m, D_MODEL), BF16),
        compiler_params=_params("arbitrary", "arbitrary"),
        name="merge",
    )(attn, conv, wa, wc, proj, proj)


def _outproj_kernel(a_ref, w_ref, x_ref, m_ref, o_ref, *, gate_row):
    y = jnp.dot(a_ref[...], w_ref[...], preferred_element_type=F32)
    o_ref[...] = x_ref[...] + (1.0 + m_ref[0, gate_row:gate_row + 1, :]) * y


def _outproj(a, w, x, mod_l, gate_row):
    m, k = a.shape
    n = w.shape[1]
    tm, tn = 1024, 512
    return pl.pallas_call(
        functools.partial(_outproj_kernel, gate_row=gate_row),
        grid=(m // tm, n // tn),
        in_specs=[
            pl.BlockSpec((tm, k), lambda i, j: (i, 0)),
            pl.BlockSpec((k, tn), lambda i, j: (0, j)),
            pl.BlockSpec((tm, tn), lambda i, j: (i, j)),
            pl.BlockSpec((1, N_MOD, tn), lambda i, j: (i * tm // SEQ, 0, j)),
        ],
        out_specs=pl.BlockSpec((tm, tn), lambda i, j: (i, j)),
        out_shape=jax.ShapeDtypeStruct((m, n), F32),
        compiler_params=_params("arbitrary", "arbitrary"),
        name="outproj",
    )(a, w, x, mod_l)


def _mlp_in_kernel(h_ref, w_ref, o_ref):
    y = jnp.maximum(jnp.dot(h_ref[...], w_ref[...], preferred_element_type=F32), 0.0)
    o_ref[...] = (y * y).astype(BF16)


def _mlp_in(h, w):
    m, d = h.shape
    n = w.shape[1]
    tm, tn = 1024, 1024
    return pl.pallas_call(
        _mlp_in_kernel,
        grid=(m // tm, n // tn),
        in_specs=[pl.BlockSpec((tm, d), lambda i, j: (i, 0)), pl.BlockSpec((d, tn), lambda i, j: (0, j))],
        out_specs=pl.BlockSpec((tm, tn), lambda i, j: (i, j)),
        out_shape=jax.ShapeDtypeStruct((m, n), BF16),
        compiler_params=_params("arbitrary", "arbitrary"),
        name="mlp_in",
    )(h, w)


def _mlp_out_kernel(a_ref, w_ref, x_ref, m_ref, o_ref, acc_ref, *, gate_row):
    k = pl.program_id(2)

    @pl.when(k == 0)
    def _():
        acc_ref[...] = jnp.zeros_like(acc_ref)

    acc_ref[...] += jnp.dot(a_ref[...], w_ref[...], preferred_element_type=F32)

    @pl.when(k == pl.num_programs(2) - 1)
    def _():
        o_ref[...] = x_ref[...] + (1.0 + m_ref[0, gate_row:gate_row + 1, :]) * acc_ref[...]


def _mlp_out(a, w, x, mod_l, gate_row):
    m, kdim = a.shape
    n = w.shape[1]
    tm, tn, tk = 1024, 1024, 2048
    return pl.pallas_call(
        functools.partial(_mlp_out_kernel, gate_row=gate_row),
        grid=(m // tm, n // tn, kdim // tk),
        in_specs=[
            pl.BlockSpec((tm, tk), lambda i, j, k: (i, k)),
            pl.BlockSpec((tk, tn), lambda i, j, k: (k, j)),
            pl.BlockSpec((tm, tn), lambda i, j, k: (i, j)),
            pl.BlockSpec((1, N_MOD, tn), lambda i, j, k: (i * tm // SEQ, 0, j)),
        ],
        out_specs=pl.BlockSpec((tm, tn), lambda i, j, k: (i, j)),
        out_shape=jax.ShapeDtypeStruct((m, n), F32),
        scratch_shapes=[pltpu.VMEM((tm, tn), F32)],
        compiler_params=_params("arbitrary", "arbitrary", "arbitrary"),
        name="mlp_out",
    )(a, w, x, mod_l)


def _rope_tables(seq):
    rows = seq // GRID_W
    row = jnp.repeat(jnp.arange(rows, dtype=F32), GRID_W)
    col = jnp.tile(jnp.arange(GRID_W, dtype=F32), rows)
    half = HEAD_DIM // 2
    freqs = ROPE_THETA ** (-jnp.arange(0, half, 2, dtype=F32) / half)
    ang_r = row[:, None] * freqs
    ang_c = col[:, None] * freqs
    zeros = jnp.zeros_like(ang_r)
    cos = jnp.concatenate([jnp.cos(ang_r), jnp.cos(ang_r), jnp.cos(ang_c), jnp.cos(ang_c)], axis=1)
    sin_lo = jnp.concatenate([-jnp.sin(ang_r), zeros, -jnp.sin(ang_c), zeros], axis=1)
    sin_hi = jnp.concatenate([zeros, jnp.sin(ang_r), zeros, jnp.sin(ang_c)], axis=1)
    return cos, sin_lo, sin_hi


def _reorder_in_cols(w):
    kv0 = ATTN_WIDTH
    kv1 = ATTN_WIDTH + 2 * KV_WIDTH
    return jnp.concatenate([w[:, :kv0], w[:, kv1:], w[:, kv0:kv1]], axis=1)


def kernel(x, c, mod_w, mod_b, norm1_g, norm2_g, w_in, q_norm_g, k_norm_g, conv_w,
           w_branch_attn, w_branch_conv, w_out, w_mlp_in, w_mlp_out, final_norm_g):
    batch, seq, d = x.shape
    depth = mod_w.shape[0]
    assert (seq, d) == (SEQ, D_MODEL) and batch <= MOD_ROWS
    m = batch * seq

    c_pad = jnp.zeros((MOD_ROWS, d), F32).at[:batch].set(c)
    mod = _modulation(c_pad, mod_w, mod_b)
    mod = mod[:, :batch].reshape(depth, batch, N_MOD, d)
    cos, sin_lo, sin_hi = _rope_tables(seq)

    xf = x.reshape(m, d)
    for l in range(depth):
        mod_l = mod[l]
        h = _norm_mod(xf, norm1_g[l], mod_l, sh_row=0, sc_row=1)
        proj = _inproj(h, _reorder_in_cols(w_in[l]).astype(BF16), q_norm_g[l], k_norm_g[l],
                       cos, sin_lo, sin_hi)
        attn = _attention(proj, batch)
        conv = _gated_conv(proj, conv_w[l])
        merged = _merge(attn, conv, w_branch_attn[l].astype(BF16), w_branch_conv[l].astype(BF16), proj)
        xf = _outproj(merged, w_out[l].astype(BF16), xf, mod_l, gate_row=2)
        h = _norm_mod(xf, norm2_g[l], mod_l, sh_row=3, sc_row=4)
        a = _mlp_in(h, w_mlp_in[l].astype(BF16))
        xf = _mlp_out(a, w_mlp_out[l].astype(BF16), xf, mod_l, gate_row=5)
    return _final_norm(xf, final_norm_g).reshape(batch, seq, d)
```

```python
import functools
import math

import jax
import jax.numpy as jnp
from jax import lax
from jax.experimental import pallas as pl
from jax.experimental.pallas import tpu as pltpu

F32 = jnp.float32
BF16 = jnp.bfloat16

D_MODEL = 4096
SEQ = 4096
HEAD_DIM = 128
N_Q_HEADS = 16
N_KV_HEADS = 4
Q_GROUP = N_Q_HEADS // N_KV_HEADS
ATTN_WIDTH = N_Q_HEADS * HEAD_DIM
KV_WIDTH = N_KV_HEADS * HEAD_DIM
CONV_WIDTH = D_MODEL // 2
D_FF = 4 * D_MODEL
GRID_W = 64
ROPE_THETA = 10000.0
NORM_EPS = 1e-6
N_MOD = 6
LOG2_E = math.log2(math.e)

COL_KV = ATTN_WIDTH
COL_REST = COL_KV + 2 * KV_WIDTH
REST_U = 0
REST_GB = REST_U + CONV_WIDTH
REST_GC = REST_GB + CONV_WIDTH
REST_GA = REST_GC + CONV_WIDTH
REST_GV = REST_GA + D_MODEL
REST_WIDTH = REST_GV + D_MODEL
IN_WIDTH = COL_REST + REST_WIDTH

V7X_VMEM_LIMIT_BYTES = 58 * 1024 * 1024
MOD_ROWS = 16


def _params(*sem):
    return pltpu.CompilerParams(dimension_semantics=sem, vmem_limit_bytes=V7X_VMEM_LIMIT_BYTES)


def _mod_kernel(c_ref, w_ref, b_ref, o_ref):
    c = c_ref[...]
    a = (c * jax.nn.sigmoid(c)).astype(BF16)
    w = w_ref[0].astype(BF16)
    o_ref[0] = jnp.dot(a, w, preferred_element_type=F32) + b_ref[0]


def _modulation(c_pad, mod_w, mod_b):
    depth, d, n = mod_w.shape
    tn = 512
    return pl.pallas_call(
        _mod_kernel,
        grid=(depth, n // tn),
        in_specs=[
            pl.BlockSpec((MOD_ROWS, d), lambda l, j: (0, 0)),
            pl.BlockSpec((1, d, tn), lambda l, j: (l, 0, j)),
            pl.BlockSpec((1, 1, tn), lambda l, j: (l, 0, j)),
        ],
        out_specs=pl.BlockSpec((1, MOD_ROWS, tn), lambda l, j: (l, 0, j)),
        out_shape=jax.ShapeDtypeStruct((depth, MOD_ROWS, n), F32),
        compiler_params=_params("arbitrary", "arbitrary"),
        name="modulation",
    )(c_pad, mod_w, mod_b.reshape(depth, 1, n))


def _rms(x):
    return x * lax.rsqrt(jnp.mean(x * x, axis=-1, keepdims=True) + NORM_EPS)


def _norm_mod_kernel(x_ref, g_ref, m_ref, o_ref, *, sh_row, sc_row):
    y = _rms(x_ref[...]) * g_ref[...]
    scale = 1.0 + m_ref[0, sc_row:sc_row + 1, :]
    shift = m_ref[0, sh_row:sh_row + 1, :]
    o_ref[...] = (y * scale + shift).astype(BF16)


def _norm_mod(x, g, mod_l, sh_row, sc_row):
    m, d = x.shape
    tr = 256
    return pl.pallas_call(
        functools.partial(_norm_mod_kernel, sh_row=sh_row, sc_row=sc_row),
        grid=(m // tr,),
        in_specs=[
            pl.BlockSpec((tr, d), lambda i: (i, 0)),
            pl.BlockSpec((1, d), lambda i: (0, 0)),
            pl.BlockSpec((1, N_MOD, d), lambda i: (i * tr // SEQ, 0, 0)),
        ],
        out_specs=pl.BlockSpec((tr, d), lambda i: (i, 0)),
        out_shape=jax.ShapeDtypeStruct((m, d), BF16),
        compiler_params=_params("arbitrary"),
        name="norm_mod",
    )(x, g.reshape(1, d), mod_l)


def _final_norm_kernel(x_ref, g_ref, o_ref):
    o_ref[...] = _rms(x_ref[...]) * g_ref[...]


def _final_norm(x, g):
    m, d = x.shape
    tr = 256
    return pl.pallas_call(
        _final_norm_kernel,
        grid=(m // tr,),
        in_specs=[pl.BlockSpec((tr, d), lambda i: (i, 0)), pl.BlockSpec((1, d), lambda i: (0, 0))],
        out_specs=pl.BlockSpec((tr, d), lambda i: (i, 0)),
        out_shape=jax.ShapeDtypeStruct((m, d), F32),
        compiler_params=_params("arbitrary"),
        name="final_norm",
    )(x, g.reshape(1, d))


CAST_ROWS = 256


def _cast_weight(w_ref, wb_ref):
    def body(r, carry):
        rows = pl.ds(pl.multiple_of(r * CAST_ROWS, CAST_ROWS), CAST_ROWS)
        wb_ref[rows, :] = w_ref[0, rows, :].astype(BF16)
        return carry
    lax.fori_loop(0, wb_ref.shape[0] // CAST_ROWS, body, 0)


def _head_norm_rope(x, g, cos, sin_lo, sin_hi):
    y = _rms(x) * g
    return (y * cos + pltpu.roll(y, HEAD_DIM - 32, axis=1) * sin_lo
            + pltpu.roll(y, 32, axis=1) * sin_hi)


INPROJ_TM, INPROJ_TN = 512, 1024


def _inproj_heads_kernel(h_ref, w_ref, g_ref, cos_ref, slo_ref, shi_ref, o_ref, wb_ref, *,
                         n_rope_heads, gain_scale):
    @pl.when(pl.program_id(1) == 0)
    def _():
        _cast_weight(w_ref, wb_ref)

    acc = jnp.dot(h_ref[...], wb_ref[...], preferred_element_type=F32)
    cos, slo, shi = cos_ref[...], slo_ref[...], shi_ref[...]
    g = g_ref[...] * gain_scale
    for h in range(n_rope_heads):
        sl = slice(h * HEAD_DIM, (h + 1) * HEAD_DIM)
        o_ref[:, sl] = _head_norm_rope(acc[:, sl], g, cos, slo, shi).astype(BF16)
    plain = n_rope_heads * HEAD_DIM
    if plain < o_ref.shape[1]:
        o_ref[:, plain:] = acc[:, plain:].astype(BF16)


def _inproj_plain_kernel(h_ref, w_ref, o_ref, wb_ref):
    @pl.when(pl.program_id(1) == 0)
    def _():
        _cast_weight(w_ref, wb_ref)

    o_ref[...] = jnp.dot(h_ref[...], wb_ref[...], preferred_element_type=F32).astype(BF16)


def _inproj(h, w_all, layer, col0, width, gain=None, tables=None, n_rope_heads=0, gain_scale=1.0):
    m, d = h.shape
    tm, tn = INPROJ_TM, INPROJ_TN
    assert col0 % tn == 0 and width % tn == 0 and SEQ % tm == 0
    in_specs = [
        pl.BlockSpec((tm, d), lambda j, i: (i, 0)),
        pl.BlockSpec((1, d, tn), lambda j, i: (layer, 0, col0 // tn + j)),
    ]
    args = [h, w_all]
    if gain is None:
        body = _inproj_plain_kernel
    else:
        body = functools.partial(_inproj_heads_kernel, n_rope_heads=n_rope_heads, gain_scale=gain_scale)
        tab = pl.BlockSpec((tm, HEAD_DIM), lambda j, i: (i % (SEQ // tm), 0))
        in_specs += [pl.BlockSpec((1, HEAD_DIM), lambda j, i: (0, 0)), tab, tab, tab]
        args += [gain.reshape(1, HEAD_DIM), *tables]
    return pl.pallas_call(
        body,
        grid=(width // tn, m // tm),
        in_specs=in_specs,
        out_specs=pl.BlockSpec((tm, tn), lambda j, i: (i, j)),
        out_shape=jax.ShapeDtypeStruct((m, width), BF16),
        scratch_shapes=[pltpu.VMEM((d, tn), BF16)],
        compiler_params=_params("arbitrary", "arbitrary"),
        name="inproj_plain" if gain is None else f"inproj_rope{n_rope_heads}",
    )(*args)


def _attn_kernel(q_ref, k_ref, v_ref, o_ref, v1_ref):
    @pl.when(pl.program_id(2) == 0)
    def _():
        v1_ref[:, :HEAD_DIM] = v_ref[...]
        v1_ref[:, HEAD_DIM:] = jnp.ones((SEQ, HEAD_DIM), BF16)

    k = k_ref[...]
    v1 = v1_ref[...]
    heads = [slice(g * HEAD_DIM, (g + 1) * HEAD_DIM) for g in range(Q_GROUP)]
    scores = [lax.dot_general(q_ref[:, sl], k, (((1,), (1,)), ((), ())), preferred_element_type=F32)
              for sl in heads]
    for sl, s in zip(heads, scores):
        p = jnp.exp2(s - jnp.max(s, axis=-1, keepdims=True))
        o = jnp.dot(p.astype(BF16), v1, preferred_element_type=F32)
        o_ref[:, sl] = (o[:, :HEAD_DIM] / o[:, HEAD_DIM:HEAD_DIM + 1]).astype(BF16)


def _attention(q, kv, batch):
    m = q.shape[0]
    tq = 256
    nq = SEQ // tq
    gw = Q_GROUP * HEAD_DIM
    return pl.pallas_call(
        _attn_kernel,
        grid=(batch, N_KV_HEADS, nq),
        in_specs=[
            pl.BlockSpec((tq, gw), lambda b, h, i: (b * nq + i, h)),
            pl.BlockSpec((SEQ, HEAD_DIM), lambda b, h, i: (b, h)),
            pl.BlockSpec((SEQ, HEAD_DIM), lambda b, h, i: (b, N_KV_HEADS + h)),
        ],
        out_specs=pl.BlockSpec((tq, gw), lambda b, h, i: (b * nq + i, h)),
        out_shape=jax.ShapeDtypeStruct((m, ATTN_WIDTH), BF16),
        scratch_shapes=[pltpu.VMEM((SEQ, 2 * HEAD_DIM), BF16)],
        compiler_params=_params("arbitrary", "arbitrary", "arbitrary"),
        name="attention",
    )(q, kv, kv)


def _conv_kernel(u_ref, gb_ref, gc_ref, up_ref, gcp_ref, un_ref, gcn_ref, w_ref, o_ref, *, tr):
    i = pl.program_id(0)
    cu = gc_ref[...].astype(F32) * u_ref[...].astype(F32)
    starts_seq = (i * tr) % SEQ == 0
    ends_seq = ((i + 1) * tr) % SEQ == 0
    prev_row = gcp_ref[7:8, :].astype(F32) * up_ref[7:8, :].astype(F32)
    next_row = gcn_ref[0:1, :].astype(F32) * un_ref[0:1, :].astype(F32)
    prev_row = jnp.where(starts_seq, 0.0, prev_row)
    next_row = jnp.where(ends_seq, 0.0, next_row)
    row = lax.broadcasted_iota(jnp.int32, cu.shape, 0)
    cu_prev = jnp.where(row == 0, prev_row, pltpu.roll(cu, 1, axis=0))
    cu_next = jnp.where(row == tr - 1, next_row, pltpu.roll(cu, tr - 1, axis=0))
    y = w_ref[0, 0:1, :] * cu_prev + w_ref[0, 1:2, :] * cu + w_ref[0, 2:3, :] * cu_next
    o_ref[...] = (gb_ref[...].astype(F32) * y).astype(BF16)


def _gated_conv(proj, conv_w_all, layer):
    m = proj.shape[0]
    tr, tc = 512, 512
    nrb = m // 8
    cu, cgb, cgc = REST_U // tc, REST_GB // tc, REST_GC // tc
    main = lambda off: pl.BlockSpec((tr, tc), lambda i, j: (i, off + j))
    prev = lambda off: pl.BlockSpec((8, tc), lambda i, j: (jnp.maximum(i * (tr // 8) - 1, 0), off + j))
    nxt = lambda off: pl.BlockSpec((8, tc), lambda i, j: (jnp.minimum((i + 1) * (tr // 8), nrb - 1), off + j))
    return pl.pallas_call(
        functools.partial(_conv_kernel, tr=tr),
        grid=(m // tr, CONV_WIDTH // tc),
        in_specs=[main(cu), main(cgb), main(cgc), prev(cu), prev(cgc), nxt(cu), nxt(cgc),
                  pl.BlockSpec((1, 3, tc), lambda i, j: (layer, 0, j))],
        out_specs=pl.BlockSpec((tr, tc), lambda i, j: (i, j)),
        out_shape=jax.ShapeDtypeStruct((m, CONV_WIDTH), BF16),
        compiler_params=_params("arbitrary", "arbitrary"),
        name="gated_conv",
    )(proj, proj, proj, proj, proj, proj, proj, conv_w_all)


def _merge_kernel(a_ref, c_ref, wa_ref, wc_ref, ga_ref, gv_ref, o_ref, wab_ref, wcb_ref):
    @pl.when(pl.program_id(1) == 0)
    def _():
        _cast_weight(wa_ref, wab_ref)
        _cast_weight(wc_ref, wcb_ref)

    ya = jnp.dot(a_ref[...], wab_ref[...], preferred_element_type=F32)
    yc = jnp.dot(c_ref[...], wcb_ref[...], preferred_element_type=F32)
    ga = jax.nn.sigmoid(ga_ref[...].astype(F32))
    gv = jax.nn.sigmoid(gv_ref[...].astype(F32))
    o_ref[...] = (ga * ya + gv * yc).astype(BF16)


def _merge(attn, conv, wa_all, wc_all, layer, proj):
    m = attn.shape[0]
    tm, tn = 512, 1024
    once = pl.Buffered(1)
    return pl.pallas_call(
        _merge_kernel,
        grid=(D_MODEL // tn, m // tm),
        in_specs=[
            pl.BlockSpec((tm, ATTN_WIDTH), lambda j, i: (i, 0)),
            pl.BlockSpec((tm, CONV_WIDTH), lambda j, i: (i, 0)),
            pl.BlockSpec((1, ATTN_WIDTH, tn), lambda j, i: (layer, 0, j), pipeline_mode=once),
            pl.BlockSpec((1, CONV_WIDTH, tn), lambda j, i: (layer, 0, j), pipeline_mode=once),
            pl.BlockSpec((tm, tn), lambda j, i: (i, REST_GA // tn + j)),
            pl.BlockSpec((tm, tn), lambda j, i: (i, REST_GV // tn + j)),
        ],
        out_specs=pl.BlockSpec((tm, tn), lambda j, i: (i, j)),
        out_shape=jax.ShapeDtypeStruct((m, D_MODEL), BF16),
        scratch_shapes=[pltpu.VMEM((ATTN_WIDTH, tn), BF16), pltpu.VMEM((CONV_WIDTH, tn), BF16)],
        compiler_params=_params("arbitrary", "arbitrary"),
        name="merge",
    )(attn, conv, wa_all, wc_all, proj, proj)


def _outproj_kernel(a_ref, w_ref, x_ref, m_ref, o_ref, wb_ref, *, gate_row):
    @pl.when(pl.program_id(1) == 0)
    def _():
        _cast_weight(w_ref, wb_ref)

    y = jnp.dot(a_ref[...], wb_ref[...], preferred_element_type=F32)
    o_ref[...] = x_ref[...] + (1.0 + m_ref[0, gate_row:gate_row + 1, :]) * y


def _outproj(a, w_all, layer, x, mod_l, gate_row):
    m, k = a.shape
    n = w_all.shape[2]
    tm, tn = 512, 1024
    return pl.pallas_call(
        functools.partial(_outproj_kernel, gate_row=gate_row),
        grid=(n // tn, m // tm),
        in_specs=[
            pl.BlockSpec((tm, k), lambda j, i: (i, 0)),
            pl.BlockSpec((1, k, tn), lambda j, i: (layer, 0, j), pipeline_mode=pl.Buffered(1)),
            pl.BlockSpec((tm, tn), lambda j, i: (i, j)),
            pl.BlockSpec((1, N_MOD, tn), lambda j, i: (i * tm // SEQ, 0, j)),
        ],
        out_specs=pl.BlockSpec((tm, tn), lambda j, i: (i, j)),
        out_shape=jax.ShapeDtypeStruct((m, n), F32),
        scratch_shapes=[pltpu.VMEM((k, tn), BF16)],
        compiler_params=_params("arbitrary", "arbitrary"),
        name="outproj",
    )(a, w_all, x, mod_l)


def _mlp_in_kernel(h_ref, w_ref, w2_ref, o_ref, w2b_ref, wb_ref):
    @pl.when(pl.program_id(1) == 0)
    def _():
        _cast_weight(w_ref, wb_ref)

    w2b_ref[...] = w2_ref[0].astype(BF16)
    y = jnp.maximum(jnp.dot(h_ref[...], wb_ref[...], preferred_element_type=F32), 0.0)
    o_ref[...] = (y * y).astype(BF16)


def _mlp_in(h, w_all, w2_all, layer):
    m, d = h.shape
    n = w_all.shape[2]
    tm, tn = 512, 1024
    nm = m // tm
    slab = w2_all.shape[1] // ((n // tn) * nm)
    assert slab * (n // tn) * nm == w2_all.shape[1] and slab % 16 == 0
    return pl.pallas_call(
        _mlp_in_kernel,
        grid=(n // tn, nm),
        in_specs=[
            pl.BlockSpec((tm, d), lambda j, i: (i, 0)),
            pl.BlockSpec((1, d, tn), lambda j, i: (layer, 0, j)),
            pl.BlockSpec((1, slab, w2_all.shape[2]), lambda j, i: (layer, j * nm + i, 0)),
        ],
        out_specs=[
            pl.BlockSpec((tm, tn), lambda j, i: (i, j)),
            pl.BlockSpec((slab, w2_all.shape[2]), lambda j, i: (j * nm + i, 0)),
        ],
        out_shape=[jax.ShapeDtypeStruct((m, n), BF16),
                   jax.ShapeDtypeStruct(w2_all.shape[1:], BF16)],
        scratch_shapes=[pltpu.VMEM((d, tn), BF16)],
        compiler_params=_params("arbitrary", "arbitrary"),
        name="mlp_in",
    )(h, w_all, w2_all)


def _mlp_out_kernel(a_ref, w_ref, x_ref, m_ref, o_ref, acc_ref, *, gate_row):
    k = pl.program_id(2)

    @pl.when(k == 0)
    def _():
        acc_ref[...] = jnp.zeros_like(acc_ref)

    acc_ref[...] += jnp.dot(a_ref[...], w_ref[...], preferred_element_type=F32)

    @pl.when(k == pl.num_programs(2) - 1)
    def _():
        o_ref[...] = x_ref[...] + (1.0 + m_ref[0, gate_row:gate_row + 1, :]) * acc_ref[...]


def _mlp_out(a, w, x, mod_l, gate_row):
    m, kdim = a.shape
    n = w.shape[1]
    tm, tn, tk = 1024, 1024, 2048
    return pl.pallas_call(
        functools.partial(_mlp_out_kernel, gate_row=gate_row),
        grid=(m // tm, n // tn, kdim // tk),
        in_specs=[
            pl.BlockSpec((tm, tk), lambda i, j, k: (i, k)),
            pl.BlockSpec((tk, tn), lambda i, j, k: (k, j)),
            pl.BlockSpec((tm, tn), lambda i, j, k: (i, j)),
            pl.BlockSpec((1, N_MOD, tn), lambda i, j, k: (i * tm // SEQ, 0, j)),
        ],
        out_specs=pl.BlockSpec((tm, tn), lambda i, j, k: (i, j)),
        out_shape=jax.ShapeDtypeStruct((m, n), F32),
        scratch_shapes=[pltpu.VMEM((tm, tn), F32)],
        compiler_params=_params("arbitrary", "arbitrary", "arbitrary"),
        name="mlp_out",
    )(a, w, x, mod_l)


def _rope_tables(seq):
    rows = seq // GRID_W
    row = jnp.repeat(jnp.arange(rows, dtype=F32), GRID_W)
    col = jnp.tile(jnp.arange(GRID_W, dtype=F32), rows)
    half = HEAD_DIM // 2
    freqs = ROPE_THETA ** (-jnp.arange(0, half, 2, dtype=F32) / half)
    ang_r = row[:, None] * freqs
    ang_c = col[:, None] * freqs
    zeros = jnp.zeros_like(ang_r)
    cos = jnp.concatenate([jnp.cos(ang_r), jnp.cos(ang_r), jnp.cos(ang_c), jnp.cos(ang_c)], axis=1)
    sin_lo = jnp.concatenate([-jnp.sin(ang_r), zeros, -jnp.sin(ang_c), zeros], axis=1)
    sin_hi = jnp.concatenate([zeros, jnp.sin(ang_r), zeros, jnp.sin(ang_c)], axis=1)
    return cos, sin_lo, sin_hi


def kernel(x, c, mod_w, mod_b, norm1_g, norm2_g, w_in, q_norm_g, k_norm_g, conv_w,
           w_branch_attn, w_branch_conv, w_out, w_mlp_in, w_mlp_out, final_norm_g):
    batch, seq, d = x.shape
    depth = mod_w.shape[0]
    assert (seq, d) == (SEQ, D_MODEL) and batch <= MOD_ROWS and w_in.shape[2] == IN_WIDTH
    m = batch * seq

    c_pad = jnp.zeros((MOD_ROWS, d), F32).at[:batch].set(c)
    mod = _modulation(c_pad, mod_w, mod_b)
    mod = mod[:, :batch].reshape(depth, batch, N_MOD, d)
    cos, sin_lo, sin_hi = _rope_tables(seq)

    xf = x.reshape(m, d)
    for l in range(depth):
        mod_l = mod[l]
        h = _norm_mod(xf, norm1_g[l], mod_l, sh_row=0, sc_row=1)
        tables = (cos, sin_lo, sin_hi)
        q = _inproj(h, w_in, l, 0, ATTN_WIDTH, q_norm_g[l], tables,
                    n_rope_heads=INPROJ_TN // HEAD_DIM, gain_scale=HEAD_DIM ** -0.5 * LOG2_E)
        kv = _inproj(h, w_in, l, COL_KV, 2 * KV_WIDTH, k_norm_g[l], tables, n_rope_heads=N_KV_HEADS)
        proj = _inproj(h, w_in, l, COL_REST, REST_WIDTH)
        attn = _attention(q, kv, batch)
        conv = _gated_conv(proj, conv_w, l)
        merged = _merge(attn, conv, w_branch_attn, w_branch_conv, l, proj)
        xf = _outproj(merged, w_out, l, xf, mod_l, gate_row=2)
        h = _norm_mod(xf, norm2_g[l], mod_l, sh_row=3, sc_row=4)
        a, w2b = _mlp_in(h, w_mlp_in, w_mlp_out, l)
        xf = _mlp_out(a, w2b, xf, mod_l, gate_row=5)
    return _final_norm(xf, final_norm_g).reshape(batch, seq, d)
```

```python
import functools
import math

import jax
import jax.numpy as jnp
from jax import lax
from jax.experimental import pallas as pl
from jax.experimental.pallas import tpu as pltpu

F32 = jnp.float32
BF16 = jnp.bfloat16

D_MODEL = 4096
SEQ = 4096
HEAD_DIM = 128
N_Q_HEADS = 16
N_KV_HEADS = 4
Q_GROUP = N_Q_HEADS // N_KV_HEADS
ATTN_WIDTH = N_Q_HEADS * HEAD_DIM
KV_WIDTH = N_KV_HEADS * HEAD_DIM
CONV_WIDTH = D_MODEL // 2
D_FF = 4 * D_MODEL
GRID_W = 64
ROPE_THETA = 10000.0
NORM_EPS = 1e-6
N_MOD = 6
LOG2_E = math.log2(math.e)

COL_KV = ATTN_WIDTH
COL_REST = COL_KV + 2 * KV_WIDTH
REST_U = 0
REST_GB = REST_U + CONV_WIDTH
REST_GC = REST_GB + CONV_WIDTH
REST_GA = REST_GC + CONV_WIDTH
REST_GV = REST_GA + D_MODEL
REST_WIDTH = REST_GV + D_MODEL
IN_WIDTH = COL_REST + REST_WIDTH

V7X_VMEM_LIMIT_BYTES = 58 * 1024 * 1024
MOD_ROWS = 16


def _params(*sem, flags=None):
    return pltpu.CompilerParams(dimension_semantics=sem, vmem_limit_bytes=V7X_VMEM_LIMIT_BYTES, flags=flags)


def _mod_kernel(c_ref, w_ref, b_ref, o_ref):
    c = c_ref[...]
    a = (c * jax.nn.sigmoid(c)).astype(BF16)
    w = w_ref[0].astype(BF16)
    o_ref[0] = jnp.dot(a, w, preferred_element_type=F32) + b_ref[0]


def _modulation(c_pad, mod_w, mod_b):
    depth, d, n = mod_w.shape
    tn = 512
    return pl.pallas_call(
        _mod_kernel,
        grid=(depth, n // tn),
        in_specs=[
            pl.BlockSpec((MOD_ROWS, d), lambda l, j: (0, 0)),
            pl.BlockSpec((1, d, tn), lambda l, j: (l, 0, j)),
            pl.BlockSpec((1, 1, tn), lambda l, j: (l, 0, j)),
        ],
        out_specs=pl.BlockSpec((1, MOD_ROWS, tn), lambda l, j: (l, 0, j)),
        out_shape=jax.ShapeDtypeStruct((depth, MOD_ROWS, n), F32),
        compiler_params=_params("arbitrary", "arbitrary"),
        name="modulation",
    )(c_pad, mod_w, mod_b.reshape(depth, 1, n))


def _rms(x):
    return x * lax.rsqrt(jnp.mean(x * x, axis=-1, keepdims=True) + NORM_EPS)


def _norm_mod_kernel(x_ref, g_ref, m_ref, o_ref, *, sh_row, sc_row):
    y = _rms(x_ref[...]) * g_ref[...]
    scale = 1.0 + m_ref[0, sc_row:sc_row + 1, :]
    shift = m_ref[0, sh_row:sh_row + 1, :]
    o_ref[...] = (y * scale + shift).astype(BF16)


def _norm_mod(x, g, mod_l, sh_row, sc_row):
    m, d = x.shape
    tr = 512
    return pl.pallas_call(
        functools.partial(_norm_mod_kernel, sh_row=sh_row, sc_row=sc_row),
        grid=(m // tr,),
        in_specs=[
            pl.BlockSpec((tr, d), lambda i: (i, 0)),
            pl.BlockSpec((1, d), lambda i: (0, 0)),
            pl.BlockSpec((1, N_MOD, d), lambda i: (i * tr // SEQ, 0, 0)),
        ],
        out_specs=pl.BlockSpec((tr, d), lambda i: (i, 0)),
        out_shape=jax.ShapeDtypeStruct((m, d), BF16),
        compiler_params=_params("arbitrary"),
        name="norm_mod",
    )(x, g.reshape(1, d), mod_l)


def _final_norm_kernel(x_ref, g_ref, o_ref):
    o_ref[...] = _rms(x_ref[...]) * g_ref[...]


def _final_norm(x, g):
    m, d = x.shape
    tr = 512
    return pl.pallas_call(
        _final_norm_kernel,
        grid=(m // tr,),
        in_specs=[pl.BlockSpec((tr, d), lambda i: (i, 0)), pl.BlockSpec((1, d), lambda i: (0, 0))],
        out_specs=pl.BlockSpec((tr, d), lambda i: (i, 0)),
        out_shape=jax.ShapeDtypeStruct((m, d), F32),
        compiler_params=_params("arbitrary"),
        name="final_norm",
    )(x, g.reshape(1, d))


CAST_ROWS = 256


def _cast_weight(w_ref, wb_ref):
    def body(r, carry):
        rows = pl.ds(pl.multiple_of(r * CAST_ROWS, CAST_ROWS), CAST_ROWS)
        wb_ref[rows, :] = w_ref[0, rows, :].astype(BF16)
        return carry
    lax.fori_loop(0, wb_ref.shape[0] // CAST_ROWS, body, 0)


def _prefetched_weight_tile(w_hbm, wf_ref, wb_ref, sem, *, layer, tile0):
    j = pl.program_id(0)
    tn = wf_ref.shape[1]

    def copy(jj):
        cols = pl.ds(pl.multiple_of((tile0 + jj) * tn, tn), tn)
        return pltpu.make_async_copy(w_hbm.at[layer, :, cols], wf_ref, sem)

    @pl.when(pl.program_id(1) == 0)
    def _():
        @pl.when(j == 0)
        def _():
            copy(0).start()

        copy(j).wait()

        def body(r, carry):
            rows = pl.ds(pl.multiple_of(r * CAST_ROWS, CAST_ROWS), CAST_ROWS)
            wb_ref[rows, :] = wf_ref[rows, :].astype(BF16)
            return carry
        lax.fori_loop(0, wb_ref.shape[0] // CAST_ROWS, body, 0)

        @pl.when(j + 1 < pl.num_programs(0))
        def _():
            copy(j + 1).start()


def _weight_stream_scratch(k, tn):
    return [pltpu.VMEM((k, tn), F32), pltpu.VMEM((k, tn), BF16), pltpu.SemaphoreType.DMA(())]


def _head_norm_rope(x, g, cos, sin_lo, sin_hi):
    y = _rms(x) * g
    return (y * cos + pltpu.roll(y, HEAD_DIM - 32, axis=1) * sin_lo
            + pltpu.roll(y, 32, axis=1) * sin_hi)


INPROJ_TM, INPROJ_TN = 1024, 1024


def _inproj_heads_kernel(h_ref, w_hbm, g_ref, cos_ref, slo_ref, shi_ref, o_ref, wf_ref, wb_ref, sem, *,
                         layer, tile0, n_rope_heads, gain_scale):
    _prefetched_weight_tile(w_hbm, wf_ref, wb_ref, sem, layer=layer, tile0=tile0)
    acc = jnp.dot(h_ref[...], wb_ref[...], preferred_element_type=F32)
    cos, slo, shi = cos_ref[...], slo_ref[...], shi_ref[...]
    g = g_ref[...] * gain_scale
    for h in range(n_rope_heads):
        sl = slice(h * HEAD_DIM, (h + 1) * HEAD_DIM)
        o_ref[:, sl] = _head_norm_rope(acc[:, sl], g, cos, slo, shi).astype(BF16)
    plain = n_rope_heads * HEAD_DIM
    if plain < o_ref.shape[1]:
        o_ref[:, plain:] = acc[:, plain:].astype(BF16)


def _inproj_plain_kernel(h_ref, w_hbm, o_ref, wf_ref, wb_ref, sem, *, layer, tile0):
    _prefetched_weight_tile(w_hbm, wf_ref, wb_ref, sem, layer=layer, tile0=tile0)
    o_ref[...] = jnp.dot(h_ref[...], wb_ref[...], preferred_element_type=F32).astype(BF16)


def _inproj(h, w_all, layer, col0, width, gain=None, tables=None, n_rope_heads=0, gain_scale=1.0):
    m, d = h.shape
    tm, tn = (INPROJ_TM if gain is None else INPROJ_TM // 2), INPROJ_TN
    assert col0 % tn == 0 and width % tn == 0 and SEQ % tm == 0
    in_specs = [
        pl.BlockSpec((tm, d), lambda j, i: (i, 0)),
        pl.BlockSpec(memory_space=pl.ANY),
    ]
    args = [h, w_all]
    if gain is None:
        body = functools.partial(_inproj_plain_kernel, layer=layer, tile0=col0 // tn)
    else:
        body = functools.partial(_inproj_heads_kernel, layer=layer, tile0=col0 // tn,
                                 n_rope_heads=n_rope_heads, gain_scale=gain_scale)
        tab = pl.BlockSpec((tm, HEAD_DIM), lambda j, i: (i % (SEQ // tm), 0))
        in_specs += [pl.BlockSpec((1, HEAD_DIM), lambda j, i: (0, 0)), tab, tab, tab]
        args += [gain.reshape(1, HEAD_DIM), *tables]
    return pl.pallas_call(
        body,
        grid=(width // tn, m // tm),
        in_specs=in_specs,
        out_specs=pl.BlockSpec((tm, tn), lambda j, i: (i, j)),
        out_shape=jax.ShapeDtypeStruct((m, width), BF16),
        scratch_shapes=_weight_stream_scratch(d, tn),
        compiler_params=_params("arbitrary", "arbitrary"),
        name="inproj_plain" if gain is None else f"inproj_rope{n_rope_heads}",
    )(*args)


def _attn_kernel(q_ref, k_ref, v_ref, o_ref, v1_ref):
    @pl.when(pl.program_id(2) == 0)
    def _():
        v1_ref[:, :HEAD_DIM] = v_ref[...]
        v1_ref[:, HEAD_DIM:] = jnp.ones((SEQ, HEAD_DIM), BF16)

    k = k_ref[...]
    v1 = v1_ref[...]
    heads = [slice(g * HEAD_DIM, (g + 1) * HEAD_DIM) for g in range(Q_GROUP)]
    scores = [lax.dot_general(q_ref[:, sl], k, (((1,), (1,)), ((), ())), preferred_element_type=F32)
              for sl in heads]
    for sl, s in zip(heads, scores):
        p = jnp.exp2(s - jnp.max(s, axis=-1, keepdims=True))
        o = jnp.dot(p.astype(BF16), v1, preferred_element_type=F32)
        o_ref[:, sl] = (o[:, :HEAD_DIM] / o[:, HEAD_DIM:HEAD_DIM + 1]).astype(BF16)


def _attention(q, kv, batch):
    m = q.shape[0]
    tq = 256
    nq = SEQ // tq
    gw = Q_GROUP * HEAD_DIM
    return pl.pallas_call(
        _attn_kernel,
        grid=(batch, N_KV_HEADS, nq),
        in_specs=[
            pl.BlockSpec((tq, gw), lambda b, h, i: (b * nq + i, h)),
            pl.BlockSpec((SEQ, HEAD_DIM), lambda b, h, i: (b, h)),
            pl.BlockSpec((SEQ, HEAD_DIM), lambda b, h, i: (b, N_KV_HEADS + h)),
        ],
        out_specs=pl.BlockSpec((tq, gw), lambda b, h, i: (b * nq + i, h)),
        out_shape=jax.ShapeDtypeStruct((m, ATTN_WIDTH), BF16),
        scratch_shapes=[pltpu.VMEM((SEQ, 2 * HEAD_DIM), BF16)],
        compiler_params=_params("arbitrary", "arbitrary", "arbitrary"),
        name="attention",
    )(q, kv, kv)


def _conv_kernel(u_ref, gb_ref, gc_ref, up_ref, gcp_ref, un_ref, gcn_ref, w_ref, o_ref, *, tr):
    i = pl.program_id(0)
    cu = gc_ref[...].astype(F32) * u_ref[...].astype(F32)
    starts_seq = (i * tr) % SEQ == 0
    ends_seq = ((i + 1) * tr) % SEQ == 0
    prev_row = gcp_ref[7:8, :].astype(F32) * up_ref[7:8, :].astype(F32)
    next_row = gcn_ref[0:1, :].astype(F32) * un_ref[0:1, :].astype(F32)
    prev_row = jnp.where(starts_seq, 0.0, prev_row)
    next_row = jnp.where(ends_seq, 0.0, next_row)
    row = lax.broadcasted_iota(jnp.int32, cu.shape, 0)
    cu_prev = jnp.where(row == 0, prev_row, pltpu.roll(cu, 1, axis=0))
    cu_next = jnp.where(row == tr - 1, next_row, pltpu.roll(cu, tr - 1, axis=0))
    y = w_ref[0, 0:1, :] * cu_prev + w_ref[0, 1:2, :] * cu + w_ref[0, 2:3, :] * cu_next
    o_ref[...] = (gb_ref[...].astype(F32) * y).astype(BF16)


def _gated_conv(proj, conv_w_all, layer):
    m = proj.shape[0]
    tr, tc = 512, 1024
    nrb = m // 8
    cu, cgb, cgc = REST_U // tc, REST_GB // tc, REST_GC // tc
    main = lambda off: pl.BlockSpec((tr, tc), lambda i, j: (i, off + j))
    prev = lambda off: pl.BlockSpec((8, tc), lambda i, j: (jnp.maximum(i * (tr // 8) - 1, 0), off + j))
    nxt = lambda off: pl.BlockSpec((8, tc), lambda i, j: (jnp.minimum((i + 1) * (tr // 8), nrb - 1), off + j))
    return pl.pallas_call(
        functools.partial(_conv_kernel, tr=tr),
        grid=(m // tr, CONV_WIDTH // tc),
        in_specs=[main(cu), main(cgb), main(cgc), prev(cu), prev(cgc), nxt(cu), nxt(cgc),
                  pl.BlockSpec((1, 3, tc), lambda i, j: (layer, 0, j))],
        out_specs=pl.BlockSpec((tr, tc), lambda i, j: (i, j)),
        out_shape=jax.ShapeDtypeStruct((m, CONV_WIDTH), BF16),
        compiler_params=_params("arbitrary", "arbitrary"),
        name="gated_conv",
    )(proj, proj, proj, proj, proj, proj, proj, conv_w_all)


def _merge_kernel(a_ref, c_ref, wa_ref, wc_ref, ga_ref, gv_ref, o_ref, wab_ref, wcb_ref):
    @pl.when(pl.program_id(1) == 0)
    def _():
        _cast_weight(wa_ref, wab_ref)
        _cast_weight(wc_ref, wcb_ref)

    ya = jnp.dot(a_ref[...], wab_ref[...], preferred_element_type=F32)
    yc = jnp.dot(c_ref[...], wcb_ref[...], preferred_element_type=F32)
    ga = jax.nn.sigmoid(ga_ref[...].astype(F32))
    gv = jax.nn.sigmoid(gv_ref[...].astype(F32))
    o_ref[...] = (ga * ya + gv * yc).astype(BF16)


def _merge(attn, conv, wa_all, wc_all, layer, proj):
    m = attn.shape[0]
    tm, tn = 512, 1024
    once = pl.Buffered(1)
    return pl.pallas_call(
        _merge_kernel,
        grid=(D_MODEL // tn, m // tm),
        in_specs=[
            pl.BlockSpec((tm, ATTN_WIDTH), lambda j, i: (i, 0)),
            pl.BlockSpec((tm, CONV_WIDTH), lambda j, i: (i, 0)),
            pl.BlockSpec((1, ATTN_WIDTH, tn), lambda j, i: (layer, 0, j), pipeline_mode=once),
            pl.BlockSpec((1, CONV_WIDTH, tn), lambda j, i: (layer, 0, j), pipeline_mode=once),
            pl.BlockSpec((tm, tn), lambda j, i: (i, REST_GA // tn + j)),
            pl.BlockSpec((tm, tn), lambda j, i: (i, REST_GV // tn + j)),
        ],
        out_specs=pl.BlockSpec((tm, tn), lambda j, i: (i, j)),
        out_shape=jax.ShapeDtypeStruct((m, D_MODEL), BF16),
        scratch_shapes=[pltpu.VMEM((ATTN_WIDTH, tn), BF16), pltpu.VMEM((CONV_WIDTH, tn), BF16)],
        compiler_params=_params("arbitrary", "arbitrary"),
        name="merge",
    )(attn, conv, wa_all, wc_all, proj, proj)


def _outproj_kernel(a_ref, w_ref, x_ref, m_ref, o_ref, wb_ref, *, gate_row):
    @pl.when(pl.program_id(1) == 0)
    def _():
        _cast_weight(w_ref, wb_ref)

    y = jnp.dot(a_ref[...], wb_ref[...], preferred_element_type=F32)
    o_ref[...] = x_ref[...] + (1.0 + m_ref[0, gate_row:gate_row + 1, :]) * y


def _outproj(a, w_all, layer, x, mod_l, gate_row):
    m, k = a.shape
    n = w_all.shape[2]
    tm, tn = 512, 1024
    return pl.pallas_call(
        functools.partial(_outproj_kernel, gate_row=gate_row),
        grid=(n // tn, m // tm),
        in_specs=[
            pl.BlockSpec((tm, k), lambda j, i: (i, 0)),
            pl.BlockSpec((1, k, tn), lambda j, i: (layer, 0, j), pipeline_mode=pl.Buffered(1)),
            pl.BlockSpec((tm, tn), lambda j, i: (i, j)),
            pl.BlockSpec((1, N_MOD, tn), lambda j, i: (i * tm // SEQ, 0, j)),
        ],
        out_specs=pl.BlockSpec((tm, tn), lambda j, i: (i, j)),
        out_shape=jax.ShapeDtypeStruct((m, n), F32),
        scratch_shapes=[pltpu.VMEM((k, tn), BF16)],
        compiler_params=_params("arbitrary", "arbitrary"),
        name="outproj",
    )(a, w_all, x, mod_l)


def _mlp_in_kernel(h_ref, w_hbm, w2_ref, o_ref, w2b_ref, wf_ref, wb_ref, sem, *, layer):
    _prefetched_weight_tile(w_hbm, wf_ref, wb_ref, sem, layer=layer, tile0=0)
    w2b_ref[...] = w2_ref[0].astype(BF16)
    y = jnp.maximum(jnp.dot(h_ref[...], wb_ref[...], preferred_element_type=F32), 0.0)
    o_ref[...] = (y * y).astype(BF16)


def _mlp_in(h, w_all, w2_all, layer):
    m, d = h.shape
    n = w_all.shape[2]
    tm, tn = INPROJ_TM, INPROJ_TN
    nm = m // tm
    slab = w2_all.shape[1] // ((n // tn) * nm)
    assert slab * (n // tn) * nm == w2_all.shape[1] and slab % 16 == 0
    return pl.pallas_call(
        functools.partial(_mlp_in_kernel, layer=layer),
        grid=(n // tn, nm),
        in_specs=[
            pl.BlockSpec((tm, d), lambda j, i: (i, 0)),
            pl.BlockSpec(memory_space=pl.ANY),
            pl.BlockSpec((1, slab, w2_all.shape[2]), lambda j, i: (layer, j * nm + i, 0)),
        ],
        out_specs=[
            pl.BlockSpec((tm, tn), lambda j, i: (i, j)),
            pl.BlockSpec((slab, w2_all.shape[2]), lambda j, i: (j * nm + i, 0)),
        ],
        out_shape=[jax.ShapeDtypeStruct((m, n), BF16),
                   jax.ShapeDtypeStruct(w2_all.shape[1:], BF16)],
        scratch_shapes=_weight_stream_scratch(d, tn),
        compiler_params=_params("arbitrary", "arbitrary"),
        name="mlp_in",
    )(h, w_all, w2_all)


def _mlp_out_kernel(a_ref, w_ref, x_ref, m_ref, o_ref, *, gate_row):
    y = jnp.dot(a_ref[...], w_ref[...], preferred_element_type=F32)
    o_ref[...] = x_ref[...] + (1.0 + m_ref[0, gate_row:gate_row + 1, :]) * y


def _mlp_out(a, w, x, mod_l, gate_row):
    m, kdim = a.shape
    n = w.shape[1]
    tm, tn = 512, 256
    return pl.pallas_call(
        functools.partial(_mlp_out_kernel, gate_row=gate_row),
        grid=(m // tm, n // tn),
        in_specs=[
            pl.BlockSpec((tm, kdim), lambda i, j: (i, 0)),
            pl.BlockSpec((kdim, tn), lambda i, j: (0, j)),
            pl.BlockSpec((tm, tn), lambda i, j: (i, j)),
            pl.BlockSpec((1, N_MOD, tn), lambda i, j: (i * tm // SEQ, 0, j)),
        ],
        out_specs=pl.BlockSpec((tm, tn), lambda i, j: (i, j)),
        out_shape=jax.ShapeDtypeStruct((m, n), F32),
        compiler_params=_params("arbitrary", "arbitrary"),
        name="mlp_out",
    )(a, w, x, mod_l)


def _rope_tables(seq):
    rows = seq // GRID_W
    row = jnp.repeat(jnp.arange(rows, dtype=F32), GRID_W)
    col = jnp.tile(jnp.arange(GRID_W, dtype=F32), rows)
    half = HEAD_DIM // 2
    freqs = ROPE_THETA ** (-jnp.arange(0, half, 2, dtype=F32) / half)
    ang_r = row[:, None] * freqs
    ang_c = col[:, None] * freqs
    zeros = jnp.zeros_like(ang_r)
    cos = jnp.concatenate([jnp.cos(ang_r), jnp.cos(ang_r), jnp.cos(ang_c), jnp.cos(ang_c)], axis=1)
    sin_lo = jnp.concatenate([-jnp.sin(ang_r), zeros, -jnp.sin(ang_c), zeros], axis=1)
    sin_hi = jnp.concatenate([zeros, jnp.sin(ang_r), zeros, jnp.sin(ang_c)], axis=1)
    return cos, sin_lo, sin_hi


def kernel(x, c, mod_w, mod_b, norm1_g, norm2_g, w_in, q_norm_g, k_norm_g, conv_w,
           w_branch_attn, w_branch_conv, w_out, w_mlp_in, w_mlp_out, final_norm_g):
    batch, seq, d = x.shape
    depth = mod_w.shape[0]
    assert (seq, d) == (SEQ, D_MODEL) and batch <= MOD_ROWS and w_in.shape[2] == IN_WIDTH
    m = batch * seq

    c_pad = jnp.zeros((MOD_ROWS, d), F32).at[:batch].set(c)
    mod = _modulation(c_pad, mod_w, mod_b)
    mod = mod[:, :batch].reshape(depth, batch, N_MOD, d)
    cos, sin_lo, sin_hi = _rope_tables(seq)

    xf = x.reshape(m, d)
    for l in range(depth):
        mod_l = mod[l]
        h = _norm_mod(xf, norm1_g[l], mod_l, sh_row=0, sc_row=1)
        tables = (cos, sin_lo, sin_hi)
        q = _inproj(h, w_in, l, 0, ATTN_WIDTH, q_norm_g[l], tables,
                    n_rope_heads=INPROJ_TN // HEAD_DIM, gain_scale=HEAD_DIM ** -0.5 * LOG2_E)
        kv = _inproj(h, w_in, l, COL_KV, 2 * KV_WIDTH, k_norm_g[l], tables, n_rope_heads=N_KV_HEADS)
        proj = _inproj(h, w_in, l, COL_REST, REST_WIDTH)
        attn = _attention(q, kv, batch)
        conv = _gated_conv(proj, conv_w, l)
        merged = _merge(attn, conv, w_branch_attn, w_branch_conv, l, proj)
        xf = _outproj(merged, w_out, l, xf, mod_l, gate_row=2)
        h = _norm_mod(xf, norm2_g[l], mod_l, sh_row=3, sc_row=4)
        a, w2b = _mlp_in(h, w_mlp_in, w_mlp_out, l)
        xf = _mlp_out(a, w2b, xf, mod_l, gate_row=5)
    return _final_norm(xf, final_norm_g).reshape(batch, seq, d)
```

```python
import functools
import math

import jax
import jax.numpy as jnp
from jax import lax
from jax.experimental import pallas as pl
from jax.experimental.pallas import tpu as pltpu

F32 = jnp.float32
BF16 = jnp.bfloat16

D_MODEL = 4096
SEQ = 4096
HEAD_DIM = 128
N_Q_HEADS = 16
N_KV_HEADS = 4
Q_GROUP = N_Q_HEADS // N_KV_HEADS
ATTN_WIDTH = N_Q_HEADS * HEAD_DIM
KV_WIDTH = N_KV_HEADS * HEAD_DIM
CONV_WIDTH = D_MODEL // 2
D_FF = 4 * D_MODEL
GRID_W = 64
ROPE_THETA = 10000.0
NORM_EPS = 1e-6
N_MOD = 6
LOG2_E = math.log2(math.e)

COL_KV = ATTN_WIDTH
COL_REST = COL_KV + 2 * KV_WIDTH
REST_U = 0
REST_GB = REST_U + CONV_WIDTH
REST_GC = REST_GB + CONV_WIDTH
REST_GA = REST_GC + CONV_WIDTH
REST_GV = REST_GA + D_MODEL
REST_WIDTH = REST_GV + D_MODEL
IN_WIDTH = COL_REST + REST_WIDTH

V7X_VMEM_LIMIT_BYTES = 58 * 1024 * 1024
MOD_ROWS = 16


def _params(*sem, flags=None):
    return pltpu.CompilerParams(dimension_semantics=sem, vmem_limit_bytes=V7X_VMEM_LIMIT_BYTES, flags=flags)


def _mod_kernel(c_ref, w_ref, b_ref, o_ref):
    c = c_ref[...]
    a = (c * jax.nn.sigmoid(c)).astype(BF16)
    w = w_ref[0].astype(BF16)
    o_ref[0] = jnp.dot(a, w, preferred_element_type=F32) + b_ref[0]


def _modulation(c_pad, mod_w, mod_b):
    depth, d, n = mod_w.shape
    tn = 512
    return pl.pallas_call(
        _mod_kernel,
        grid=(depth, n // tn),
        in_specs=[
            pl.BlockSpec((MOD_ROWS, d), lambda l, j: (0, 0)),
            pl.BlockSpec((1, d, tn), lambda l, j: (l, 0, j)),
            pl.BlockSpec((1, 1, tn), lambda l, j: (l, 0, j)),
        ],
        out_specs=pl.BlockSpec((1, MOD_ROWS, tn), lambda l, j: (l, 0, j)),
        out_shape=jax.ShapeDtypeStruct((depth, MOD_ROWS, n), F32),
        compiler_params=_params("arbitrary", "arbitrary"),
        name="modulation",
    )(c_pad, mod_w, mod_b.reshape(depth, 1, n))


def _rms(x):
    return x * lax.rsqrt(jnp.mean(x * x, axis=-1, keepdims=True) + NORM_EPS)


def _norm_mod_kernel(x_ref, g_ref, m_ref, o_ref, *, sh_row, sc_row):
    y = _rms(x_ref[...]) * g_ref[...]
    scale = 1.0 + m_ref[0, sc_row:sc_row + 1, :]
    shift = m_ref[0, sh_row:sh_row + 1, :]
    o_ref[...] = (y * scale + shift).astype(BF16)


def _norm_mod(x, g, mod_l, sh_row, sc_row):
    m, d = x.shape
    tr = 512
    return pl.pallas_call(
        functools.partial(_norm_mod_kernel, sh_row=sh_row, sc_row=sc_row),
        grid=(m // tr,),
        in_specs=[
            pl.BlockSpec((tr, d), lambda i: (i, 0)),
            pl.BlockSpec((1, d), lambda i: (0, 0)),
            pl.BlockSpec((1, N_MOD, d), lambda i: (i * tr // SEQ, 0, 0)),
        ],
        out_specs=pl.BlockSpec((tr, d), lambda i: (i, 0)),
        out_shape=jax.ShapeDtypeStruct((m, d), BF16),
        compiler_params=_params("arbitrary"),
        name="norm_mod",
    )(x, g.reshape(1, d), mod_l)


def _final_norm_kernel(x_ref, g_ref, o_ref):
    o_ref[...] = _rms(x_ref[...]) * g_ref[...]


def _final_norm(x, g):
    m, d = x.shape
    tr = 512
    return pl.pallas_call(
        _final_norm_kernel,
        grid=(m // tr,),
        in_specs=[pl.BlockSpec((tr, d), lambda i: (i, 0)), pl.BlockSpec((1, d), lambda i: (0, 0))],
        out_specs=pl.BlockSpec((tr, d), lambda i: (i, 0)),
        out_shape=jax.ShapeDtypeStruct((m, d), F32),
        compiler_params=_params("arbitrary"),
        name="final_norm",
    )(x, g.reshape(1, d))


CAST_ROWS = 256


def _cast_weight(w_ref, wb_ref):
    def body(r, carry):
        rows = pl.ds(pl.multiple_of(r * CAST_ROWS, CAST_ROWS), CAST_ROWS)
        wb_ref[rows, :] = w_ref[0, rows, :].astype(BF16)
        return carry
    lax.fori_loop(0, wb_ref.shape[0] // CAST_ROWS, body, 0)


def _prefetched_weight_tile(w_hbm, wf_ref, wb_ref, sem, *, layer, tile0):
    j = pl.program_id(0)
    tn = wf_ref.shape[1]

    def copy(jj):
        cols = pl.ds(pl.multiple_of((tile0 + jj) * tn, tn), tn)
        return pltpu.make_async_copy(w_hbm.at[layer, :, cols], wf_ref, sem)

    @pl.when(pl.program_id(1) == 0)
    def _():
        @pl.when(j == 0)
        def _():
            copy(0).start()

        copy(j).wait()

        def body(r, carry):
            rows = pl.ds(pl.multiple_of(r * CAST_ROWS, CAST_ROWS), CAST_ROWS)
            wb_ref[rows, :] = wf_ref[rows, :].astype(BF16)
            return carry
        lax.fori_loop(0, wb_ref.shape[0] // CAST_ROWS, body, 0)

        @pl.when(j + 1 < pl.num_programs(0))
        def _():
            copy(j + 1).start()


def _weight_stream_scratch(k, tn):
    return [pltpu.VMEM((k, tn), F32), pltpu.VMEM((k, tn), BF16), pltpu.SemaphoreType.DMA(())]


def _head_norm_rope(x, g, cos, sin_lo, sin_hi):
    y = _rms(x) * g
    return (y * cos + pltpu.roll(y, HEAD_DIM - 32, axis=1) * sin_lo
            + pltpu.roll(y, 32, axis=1) * sin_hi)


INPROJ_TM, INPROJ_TN = 1024, 1024


def _inproj_heads_kernel(h_ref, w_hbm, g_ref, cos_ref, slo_ref, shi_ref, o_ref, wf_ref, wb_ref, sem, *,
                         layer, tile0, n_rope_heads, gain_scale):
    _prefetched_weight_tile(w_hbm, wf_ref, wb_ref, sem, layer=layer, tile0=tile0)
    acc = jnp.dot(h_ref[...], wb_ref[...], preferred_element_type=F32)
    cos, slo, shi = cos_ref[...], slo_ref[...], shi_ref[...]
    g = g_ref[...] * gain_scale
    for h in range(n_rope_heads):
        sl = slice(h * HEAD_DIM, (h + 1) * HEAD_DIM)
        o_ref[:, sl] = _head_norm_rope(acc[:, sl], g, cos, slo, shi).astype(BF16)
    plain = n_rope_heads * HEAD_DIM
    if plain < o_ref.shape[1]:
        o_ref[:, plain:] = acc[:, plain:].astype(BF16)


def _inproj_plain_kernel(h_ref, w_hbm, o_ref, wf_ref, wb_ref, sem, *, layer, tile0):
    _prefetched_weight_tile(w_hbm, wf_ref, wb_ref, sem, layer=layer, tile0=tile0)
    o_ref[...] = jnp.dot(h_ref[...], wb_ref[...], preferred_element_type=F32).astype(BF16)


def _inproj(h, w_all, layer, col0, width, gain=None, tables=None, n_rope_heads=0, gain_scale=1.0):
    m, d = h.shape
    tm, tn = (INPROJ_TM if gain is None else INPROJ_TM // 2), INPROJ_TN
    assert col0 % tn == 0 and width % tn == 0 and SEQ % tm == 0
    in_specs = [
        pl.BlockSpec((tm, d), lambda j, i: (i, 0)),
        pl.BlockSpec(memory_space=pl.ANY),
    ]
    args = [h, w_all]
    if gain is None:
        body = functools.partial(_inproj_plain_kernel, layer=layer, tile0=col0 // tn)
    else:
        body = functools.partial(_inproj_heads_kernel, layer=layer, tile0=col0 // tn,
                                 n_rope_heads=n_rope_heads, gain_scale=gain_scale)
        tab = pl.BlockSpec((tm, HEAD_DIM), lambda j, i: (i % (SEQ // tm), 0))
        in_specs += [pl.BlockSpec((1, HEAD_DIM), lambda j, i: (0, 0)), tab, tab, tab]
        args += [gain.reshape(1, HEAD_DIM), *tables]
    return pl.pallas_call(
        body,
        grid=(width // tn, m // tm),
        in_specs=in_specs,
        out_specs=pl.BlockSpec((tm, tn), lambda j, i: (i, j)),
        out_shape=jax.ShapeDtypeStruct((m, width), BF16),
        scratch_shapes=_weight_stream_scratch(d, tn),
        compiler_params=_params("arbitrary", "arbitrary"),
        name="inproj_plain" if gain is None else f"inproj_rope{n_rope_heads}",
    )(*args)


def _attn_kernel(q_ref, k_ref, v_ref, o_ref, v1_ref):
    @pl.when(pl.program_id(2) == 0)
    def _():
        v1_ref[:, :HEAD_DIM] = v_ref[...]
        v1_ref[:, HEAD_DIM:] = jnp.ones((SEQ, HEAD_DIM), BF16)

    k = k_ref[...]
    v1 = v1_ref[...]
    heads = [slice(g * HEAD_DIM, (g + 1) * HEAD_DIM) for g in range(Q_GROUP)]
    scores = [lax.dot_general(q_ref[:, sl], k, (((1,), (1,)), ((), ())), preferred_element_type=F32)
              for sl in heads]
    for sl, s in zip(heads, scores):
        p = jnp.exp2(s - jnp.max(s, axis=-1, keepdims=True))
        o = jnp.dot(p.astype(BF16), v1, preferred_element_type=F32)
        o_ref[:, sl] = (o[:, :HEAD_DIM] / o[:, HEAD_DIM:HEAD_DIM + 1]).astype(BF16)


def _attention(q, kv, batch):
    m = q.shape[0]
    tq = 512
    nq = SEQ // tq
    gw = Q_GROUP * HEAD_DIM
    return pl.pallas_call(
        _attn_kernel,
        grid=(batch, N_KV_HEADS, nq),
        in_specs=[
            pl.BlockSpec((tq, gw), lambda b, h, i: (b * nq + i, h)),
            pl.BlockSpec((SEQ, HEAD_DIM), lambda b, h, i: (b, h)),
            pl.BlockSpec((SEQ, HEAD_DIM), lambda b, h, i: (b, N_KV_HEADS + h)),
        ],
        out_specs=pl.BlockSpec((tq, gw), lambda b, h, i: (b * nq + i, h)),
        out_shape=jax.ShapeDtypeStruct((m, ATTN_WIDTH), BF16),
        scratch_shapes=[pltpu.VMEM((SEQ, 2 * HEAD_DIM), BF16)],
        compiler_params=_params("arbitrary", "arbitrary", "arbitrary"),
        name="attention",
    )(q, kv, kv)


def _conv_kernel(u_ref, gb_ref, gc_ref, up_ref, gcp_ref, un_ref, gcn_ref, w_ref, o_ref, *, tr):
    i = pl.program_id(0)
    cu = gc_ref[...].astype(F32) * u_ref[...].astype(F32)
    starts_seq = (i * tr) % SEQ == 0
    ends_seq = ((i + 1) * tr) % SEQ == 0
    prev_row = gcp_ref[7:8, :].astype(F32) * up_ref[7:8, :].astype(F32)
    next_row = gcn_ref[0:1, :].astype(F32) * un_ref[0:1, :].astype(F32)
    prev_row = jnp.where(starts_seq, 0.0, prev_row)
    next_row = jnp.where(ends_seq, 0.0, next_row)
    row = lax.broadcasted_iota(jnp.int32, cu.shape, 0)
    cu_prev = jnp.where(row == 0, prev_row, pltpu.roll(cu, 1, axis=0))
    cu_next = jnp.where(row == tr - 1, next_row, pltpu.roll(cu, tr - 1, axis=0))
    y = w_ref[0, 0:1, :] * cu_prev + w_ref[0, 1:2, :] * cu + w_ref[0, 2:3, :] * cu_next
    o_ref[...] = (gb_ref[...].astype(F32) * y).astype(BF16)


def _gated_conv(proj, conv_w_all, layer):
    m = proj.shape[0]
    tr, tc = 512, 1024
    nrb = m // 8
    cu, cgb, cgc = REST_U // tc, REST_GB // tc, REST_GC // tc
    main = lambda off: pl.BlockSpec((tr, tc), lambda i, j: (i, off + j))
    prev = lambda off: pl.BlockSpec((8, tc), lambda i, j: (jnp.maximum(i * (tr // 8) - 1, 0), off + j))
    nxt = lambda off: pl.BlockSpec((8, tc), lambda i, j: (jnp.minimum((i + 1) * (tr // 8), nrb - 1), off + j))
    return pl.pallas_call(
        functools.partial(_conv_kernel, tr=tr),
        grid=(m // tr, CONV_WIDTH // tc),
        in_specs=[main(cu), main(cgb), main(cgc), prev(cu), prev(cgc), nxt(cu), nxt(cgc),
                  pl.BlockSpec((1, 3, tc), lambda i, j: (layer, 0, j))],
        out_specs=pl.BlockSpec((tr, tc), lambda i, j: (i, j)),
        out_shape=jax.ShapeDtypeStruct((m, CONV_WIDTH), BF16),
        compiler_params=_params("arbitrary", "arbitrary"),
        name="gated_conv",
    )(proj, proj, proj, proj, proj, proj, proj, conv_w_all)


def _merge_kernel(a_ref, c_ref, wa_ref, wc_ref, ga_ref, gv_ref, o_ref, wab_ref, wcb_ref):
    @pl.when(pl.program_id(1) == 0)
    def _():
        _cast_weight(wa_ref, wab_ref)
        _cast_weight(wc_ref, wcb_ref)

    ya = jnp.dot(a_ref[...], wab_ref[...], preferred_element_type=F32)
    yc = jnp.dot(c_ref[...], wcb_ref[...], preferred_element_type=F32)
    ga = jax.nn.sigmoid(ga_ref[...].astype(F32))
    gv = jax.nn.sigmoid(gv_ref[...].astype(F32))
    o_ref[...] = (ga * ya + gv * yc).astype(BF16)


def _merge(attn, conv, wa_all, wc_all, layer, proj):
    m = attn.shape[0]
    tm, tn = 512, 1024
    once = pl.Buffered(1)
    return pl.pallas_call(
        _merge_kernel,
        grid=(D_MODEL // tn, m // tm),
        in_specs=[
            pl.BlockSpec((tm, ATTN_WIDTH), lambda j, i: (i, 0)),
            pl.BlockSpec((tm, CONV_WIDTH), lambda j, i: (i, 0)),
            pl.BlockSpec((1, ATTN_WIDTH, tn), lambda j, i: (layer, 0, j), pipeline_mode=once),
            pl.BlockSpec((1, CONV_WIDTH, tn), lambda j, i: (layer, 0, j), pipeline_mode=once),
            pl.BlockSpec((tm, tn), lambda j, i: (i, REST_GA // tn + j)),
            pl.BlockSpec((tm, tn), lambda j, i: (i, REST_GV // tn + j)),
        ],
        out_specs=pl.BlockSpec((tm, tn), lambda j, i: (i, j)),
        out_shape=jax.ShapeDtypeStruct((m, D_MODEL), BF16),
        scratch_shapes=[pltpu.VMEM((ATTN_WIDTH, tn), BF16), pltpu.VMEM((CONV_WIDTH, tn), BF16)],
        compiler_params=_params("arbitrary", "arbitrary"),
        name="merge",
    )(attn, conv, wa_all, wc_all, proj, proj)


def _outproj_kernel(a_ref, w_ref, x_ref, m_ref, o_ref, wb_ref, *, gate_row):
    @pl.when(pl.program_id(1) == 0)
    def _():
        _cast_weight(w_ref, wb_ref)

    y = jnp.dot(a_ref[...], wb_ref[...], preferred_element_type=F32)
    o_ref[...] = x_ref[...] + (1.0 + m_ref[0, gate_row:gate_row + 1, :]) * y


def _outproj(a, w_all, layer, x, mod_l, gate_row):
    m, k = a.shape
    n = w_all.shape[2]
    tm, tn = 512, 1024
    return pl.pallas_call(
        functools.partial(_outproj_kernel, gate_row=gate_row),
        grid=(n // tn, m // tm),
        in_specs=[
            pl.BlockSpec((tm, k), lambda j, i: (i, 0)),
            pl.BlockSpec((1, k, tn), lambda j, i: (layer, 0, j), pipeline_mode=pl.Buffered(1)),
            pl.BlockSpec((tm, tn), lambda j, i: (i, j)),
            pl.BlockSpec((1, N_MOD, tn), lambda j, i: (i * tm // SEQ, 0, j)),
        ],
        out_specs=pl.BlockSpec((tm, tn), lambda j, i: (i, j)),
        out_shape=jax.ShapeDtypeStruct((m, n), F32),
        scratch_shapes=[pltpu.VMEM((k, tn), BF16)],
        compiler_params=_params("arbitrary", "arbitrary"),
        name="outproj",
    )(a, w_all, x, mod_l)


def _mlp_in_kernel(h_ref, w_hbm, w2_ref, o_ref, w2b_ref, wf_ref, wb_ref, sem, *, layer):
    _prefetched_weight_tile(w_hbm, wf_ref, wb_ref, sem, layer=layer, tile0=0)
    w2b_ref[...] = w2_ref[0].astype(BF16)
    y = jnp.maximum(jnp.dot(h_ref[...], wb_ref[...], preferred_element_type=F32), 0.0)
    o_ref[...] = (y * y).astype(BF16)


def _mlp_in(h, w_all, w2_all, layer):
    m, d = h.shape
    n = w_all.shape[2]
    tm, tn = INPROJ_TM, INPROJ_TN
    nm = m // tm
    slab = w2_all.shape[1] // ((n // tn) * nm)
    assert slab * (n // tn) * nm == w2_all.shape[1] and slab % 16 == 0
    return pl.pallas_call(
        functools.partial(_mlp_in_kernel, layer=layer),
        grid=(n // tn, nm),
        in_specs=[
            pl.BlockSpec((tm, d), lambda j, i: (i, 0)),
            pl.BlockSpec(memory_space=pl.ANY),
            pl.BlockSpec((1, slab, w2_all.shape[2]), lambda j, i: (layer, j * nm + i, 0)),
        ],
        out_specs=[
            pl.BlockSpec((tm, tn), lambda j, i: (i, j)),
            pl.BlockSpec((slab, w2_all.shape[2]), lambda j, i: (j * nm + i, 0)),
        ],
        out_shape=[jax.ShapeDtypeStruct((m, n), BF16),
                   jax.ShapeDtypeStruct(w2_all.shape[1:], BF16)],
        scratch_shapes=_weight_stream_scratch(d, tn),
        compiler_params=_params("arbitrary", "arbitrary"),
        name="mlp_in",
    )(h, w_all, w2_all)


def _mlp_out_kernel(a_ref, w_ref, x_ref, m_ref, o_ref, *, gate_row):
    k = pl.program_id(2)

    @pl.when(k == 0)
    def _():
        o_ref[...] = jnp.zeros_like(o_ref)

    o_ref[...] += jnp.dot(a_ref[...], w_ref[...], preferred_element_type=F32)

    @pl.when(k == pl.num_programs(2) - 1)
    def _():
        o_ref[...] = x_ref[...] + (1.0 + m_ref[0, gate_row:gate_row + 1, :]) * o_ref[...]


def _mlp_out(a, w, x, mod_l, gate_row):
    m, kdim = a.shape
    n = w.shape[1]
    tm, tn, tk = 1024, 1024, 4096
    return pl.pallas_call(
        functools.partial(_mlp_out_kernel, gate_row=gate_row),
        grid=(m // tm, n // tn, kdim // tk),
        in_specs=[
            pl.BlockSpec((tm, tk), lambda i, j, k: (i, k)),
            pl.BlockSpec((tk, tn), lambda i, j, k: (k, j)),
            pl.BlockSpec((tm, tn), lambda i, j, k: (i, j)),
            pl.BlockSpec((1, N_MOD, tn), lambda i, j, k: (i * tm // SEQ, 0, j)),
        ],
        out_specs=pl.BlockSpec((tm, tn), lambda i, j, k: (i, j)),
        out_shape=jax.ShapeDtypeStruct((m, n), F32),
        compiler_params=_params("arbitrary", "arbitrary", "arbitrary"),
        name="mlp_out",
    )(a, w, x, mod_l)


def _rope_tables(seq):
    rows = seq // GRID_W
    row = jnp.repeat(jnp.arange(rows, dtype=F32), GRID_W)
    col = jnp.tile(jnp.arange(GRID_W, dtype=F32), rows)
    half = HEAD_DIM // 2
    freqs = ROPE_THETA ** (-jnp.arange(0, half, 2, dtype=F32) / half)
    ang_r = row[:, None] * freqs
    ang_c = col[:, None] * freqs
    zeros = jnp.zeros_like(ang_r)
    cos = jnp.concatenate([jnp.cos(ang_r), jnp.cos(ang_r), jnp.cos(ang_c), jnp.cos(ang_c)], axis=1)
    sin_lo = jnp.concatenate([-jnp.sin(ang_r), zeros, -jnp.sin(ang_c), zeros], axis=1)
    sin_hi = jnp.concatenate([zeros, jnp.sin(ang_r), zeros, jnp.sin(ang_c)], axis=1)
    return cos, sin_lo, sin_hi


def kernel(x, c, mod_w, mod_b, norm1_g, norm2_g, w_in, q_norm_g, k_norm_g, conv_w,
           w_branch_attn, w_branch_conv, w_out, w_mlp_in, w_mlp_out, final_norm_g):
    batch, seq, d = x.shape
    depth = mod_w.shape[0]
    assert (seq, d) == (SEQ, D_MODEL) and batch <= MOD_ROWS and w_in.shape[2] == IN_WIDTH
    m = batch * seq

    c_pad = jnp.zeros((MOD_ROWS, d), F32).at[:batch].set(c)
    mod = _modulation(c_pad, mod_w, mod_b)
    mod = mod[:, :batch].reshape(depth, batch, N_MOD, d)
    cos, sin_lo, sin_hi = _rope_tables(seq)

    xf = x.reshape(m, d)
    for l in range(depth):
        mod_l = mod[l]
        h = _norm_mod(xf, norm1_g[l], mod_l, sh_row=0, sc_row=1)
        tables = (cos, sin_lo, sin_hi)
        q = _inproj(h, w_in, l, 0, ATTN_WIDTH, q_norm_g[l], tables,
                    n_rope_heads=INPROJ_TN // HEAD_DIM, gain_scale=HEAD_DIM ** -0.5 * LOG2_E)
        kv = _inproj(h, w_in, l, COL_KV, 2 * KV_WIDTH, k_norm_g[l], tables, n_rope_heads=N_KV_HEADS)
        proj = _inproj(h, w_in, l, COL_REST, REST_WIDTH)
        attn = _attention(q, kv, batch)
        conv = _gated_conv(proj, conv_w, l)
        merged = _merge(attn, conv, w_branch_attn, w_branch_conv, l, proj)
        xf = _outproj(merged, w_out, l, xf, mod_l, gate_row=2)
        h = _norm_mod(xf, norm2_g[l], mod_l, sh_row=3, sc_row=4)
        a, w2b = _mlp_in(h, w_mlp_in, w_mlp_out, l)
        xf = _mlp_out(a, w2b, xf, mod_l, gate_row=5)
    return _final_norm(xf, final_norm_g).reshape(batch, seq, d)
```

```python
import functools
import math

import jax
import jax.numpy as jnp
from jax import lax
from jax.experimental import pallas as pl
from jax.experimental.pallas import tpu as pltpu

F32 = jnp.float32
BF16 = jnp.bfloat16

D_MODEL = 4096
SEQ = 4096
HEAD_DIM = 128
N_Q_HEADS = 16
N_KV_HEADS = 4
Q_GROUP = N_Q_HEADS // N_KV_HEADS
ATTN_WIDTH = N_Q_HEADS * HEAD_DIM
KV_WIDTH = N_KV_HEADS * HEAD_DIM
CONV_WIDTH = D_MODEL // 2
D_FF = 4 * D_MODEL
GRID_W = 64
ROPE_THETA = 10000.0
NORM_EPS = 1e-6
N_MOD = 6
LOG2_E = math.log2(math.e)

COL_KV = ATTN_WIDTH
COL_REST = COL_KV + 2 * KV_WIDTH
REST_U = 0
REST_GB = REST_U + CONV_WIDTH
REST_GC = REST_GB + CONV_WIDTH
REST_GA = REST_GC + CONV_WIDTH
REST_GV = REST_GA + D_MODEL
REST_WIDTH = REST_GV + D_MODEL
IN_WIDTH = COL_REST + REST_WIDTH

V7X_VMEM_LIMIT_BYTES = 58 * 1024 * 1024
MOD_ROWS = 16


def _params(*sem, vmem_limit_bytes=V7X_VMEM_LIMIT_BYTES):
    return pltpu.CompilerParams(dimension_semantics=sem, vmem_limit_bytes=vmem_limit_bytes)


def _mod_kernel(c_ref, w_ref, b_ref, o_ref):
    c = c_ref[...]
    a = (c * jax.nn.sigmoid(c)).astype(BF16)
    w = w_ref[0].astype(BF16)
    o_ref[0] = jnp.dot(a, w, preferred_element_type=F32) + b_ref[0]


def _modulation(c_pad, mod_w, mod_b):
    depth, d, n = mod_w.shape
    tn = 512
    return pl.pallas_call(
        _mod_kernel,
        grid=(depth, n // tn),
        in_specs=[
            pl.BlockSpec((MOD_ROWS, d), lambda l, j: (0, 0)),
            pl.BlockSpec((1, d, tn), lambda l, j: (l, 0, j)),
            pl.BlockSpec((1, 1, tn), lambda l, j: (l, 0, j)),
        ],
        out_specs=pl.BlockSpec((1, MOD_ROWS, tn), lambda l, j: (l, 0, j)),
        out_shape=jax.ShapeDtypeStruct((depth, MOD_ROWS, n), F32),
        compiler_params=_params("arbitrary", "arbitrary"),
        name="modulation",
    )(c_pad, mod_w, mod_b.reshape(depth, 1, n))


def _rms(x):
    return x * lax.rsqrt(jnp.mean(x * x, axis=-1, keepdims=True) + NORM_EPS)


def _norm_mod_kernel(x_ref, g_ref, m_ref, o_ref, *, sh_row, sc_row):
    y = _rms(x_ref[...]) * g_ref[...]
    scale = 1.0 + m_ref[0, sc_row:sc_row + 1, :]
    shift = m_ref[0, sh_row:sh_row + 1, :]
    o_ref[...] = (y * scale + shift).astype(BF16)


def _norm_mod(x, g, mod_l, sh_row, sc_row):
    m, d = x.shape
    tr = 512
    return pl.pallas_call(
        functools.partial(_norm_mod_kernel, sh_row=sh_row, sc_row=sc_row),
        grid=(m // tr,),
        in_specs=[
            pl.BlockSpec((tr, d), lambda i: (i, 0)),
            pl.BlockSpec((1, d), lambda i: (0, 0)),
            pl.BlockSpec((1, N_MOD, d), lambda i: (i * tr // SEQ, 0, 0)),
        ],
        out_specs=pl.BlockSpec((tr, d), lambda i: (i, 0)),
        out_shape=jax.ShapeDtypeStruct((m, d), BF16),
        compiler_params=_params("arbitrary"),
        name="norm_mod",
    )(x, g.reshape(1, d), mod_l)


def _final_norm_kernel(x_ref, g_ref, o_ref):
    o_ref[...] = _rms(x_ref[...]) * g_ref[...]


def _final_norm(x, g):
    m, d = x.shape
    tr = 512
    return pl.pallas_call(
        _final_norm_kernel,
        grid=(m // tr,),
        in_specs=[pl.BlockSpec((tr, d), lambda i: (i, 0)), pl.BlockSpec((1, d), lambda i: (0, 0))],
        out_specs=pl.BlockSpec((tr, d), lambda i: (i, 0)),
        out_shape=jax.ShapeDtypeStruct((m, d), F32),
        compiler_params=_params("arbitrary"),
        name="final_norm",
    )(x, g.reshape(1, d))


CAST_ROWS = 256


def _prefetched_weight_tile(w_hbm, wf_ref, wb_ref, sem, *, layer, tile0):
    j = pl.program_id(0)
    tn = wf_ref.shape[1]

    def copy(jj):
        cols = pl.ds(pl.multiple_of((tile0 + jj) * tn, tn), tn)
        return pltpu.make_async_copy(w_hbm.at[layer, :, cols], wf_ref, sem)

    @pl.when(pl.program_id(1) == 0)
    def _():
        @pl.when(j == 0)
        def _():
            copy(0).start()

        copy(j).wait()

        def body(r, carry):
            rows = pl.ds(pl.multiple_of(r * CAST_ROWS, CAST_ROWS), CAST_ROWS)
            wb_ref[rows, :] = wf_ref[rows, :].astype(BF16)
            return carry
        lax.fori_loop(0, wb_ref.shape[0] // CAST_ROWS, body, 0)

        @pl.when(j + 1 < pl.num_programs(0))
        def _():
            copy(j + 1).start()


def _weight_stream_scratch(k, tn):
    return [pltpu.VMEM((k, tn), F32), pltpu.VMEM((k, tn), BF16), pltpu.SemaphoreType.DMA(())]


def _head_norm_rope(x, g, cos, sin_lo, sin_hi):
    y = _rms(x) * g
    return (y * cos + pltpu.roll(y, HEAD_DIM - 32, axis=1) * sin_lo
            + pltpu.roll(y, 32, axis=1) * sin_hi)


INPROJ_TM, INPROJ_TN = 1024, 1024


def _inproj_heads_kernel(h_ref, w_hbm, g_ref, cos_ref, slo_ref, shi_ref, o_ref, wf_ref, wb_ref, sem, *,
                         layer, tile0, n_rope_heads, gain_scale):
    _prefetched_weight_tile(w_hbm, wf_ref, wb_ref, sem, layer=layer, tile0=tile0)
    acc = jnp.dot(h_ref[...], wb_ref[...], preferred_element_type=F32)
    cos, slo, shi = cos_ref[...], slo_ref[...], shi_ref[...]
    g = g_ref[...] * gain_scale
    for h in range(n_rope_heads):
        sl = slice(h * HEAD_DIM, (h + 1) * HEAD_DIM)
        o_ref[:, sl] = _head_norm_rope(acc[:, sl], g, cos, slo, shi).astype(BF16)
    plain = n_rope_heads * HEAD_DIM
    if plain < o_ref.shape[1]:
        o_ref[:, plain:] = acc[:, plain:].astype(BF16)


def _inproj_plain_kernel(h_ref, w_hbm, o_ref, wf_ref, wb_ref, sem, *, layer, tile0):
    _prefetched_weight_tile(w_hbm, wf_ref, wb_ref, sem, layer=layer, tile0=tile0)
    o_ref[...] = jnp.dot(h_ref[...], wb_ref[...], preferred_element_type=F32).astype(BF16)


def _inproj(h, w_all, layer, col0, width, gain=None, tables=None, n_rope_heads=0, gain_scale=1.0):
    m, d = h.shape
    tm, tn = (INPROJ_TM if gain is None else INPROJ_TM // 2), INPROJ_TN
    assert col0 % tn == 0 and width % tn == 0 and SEQ % tm == 0
    in_specs = [
        pl.BlockSpec((tm, d), lambda j, i: (i, 0)),
        pl.BlockSpec(memory_space=pl.ANY),
    ]
    args = [h, w_all]
    if gain is None:
        body = functools.partial(_inproj_plain_kernel, layer=layer, tile0=col0 // tn)
    else:
        body = functools.partial(_inproj_heads_kernel, layer=layer, tile0=col0 // tn,
                                 n_rope_heads=n_rope_heads, gain_scale=gain_scale)
        tab = pl.BlockSpec((tm, HEAD_DIM), lambda j, i: (i % (SEQ // tm), 0))
        in_specs += [pl.BlockSpec((1, HEAD_DIM), lambda j, i: (0, 0)), tab, tab, tab]
        args += [gain.reshape(1, HEAD_DIM), *tables]
    return pl.pallas_call(
        body,
        grid=(width // tn, m // tm),
        in_specs=in_specs,
        out_specs=pl.BlockSpec((tm, tn), lambda j, i: (i, j)),
        out_shape=jax.ShapeDtypeStruct((m, width), BF16),
        scratch_shapes=_weight_stream_scratch(d, tn),
        compiler_params=_params("arbitrary", "arbitrary"),
        name="inproj_plain" if gain is None else f"inproj_rope{n_rope_heads}",
    )(*args)


def _attn_kernel(q_ref, k_ref, v_ref, o_ref, v1_ref):
    @pl.when(pl.program_id(2) == 0)
    def _():
        v1_ref[:, :HEAD_DIM] = v_ref[...]
        v1_ref[:, HEAD_DIM:] = jnp.ones((SEQ, HEAD_DIM), BF16)

    k = k_ref[...]
    v1 = v1_ref[...]
    heads = [slice(g * HEAD_DIM, (g + 1) * HEAD_DIM) for g in range(Q_GROUP)]
    scores = [lax.dot_general(q_ref[:, sl], k, (((1,), (1,)), ((), ())), preferred_element_type=F32)
              for sl in heads]
    for sl, s in zip(heads, scores):
        p = jnp.exp2(s - jnp.max(s, axis=-1, keepdims=True))
        o = jnp.dot(p.astype(BF16), v1, preferred_element_type=F32)
        o_ref[:, sl] = (o[:, :HEAD_DIM] / o[:, HEAD_DIM:HEAD_DIM + 1]).astype(BF16)


def _attention(q, kv, batch):
    m = q.shape[0]
    tq = 512
    nq = SEQ // tq
    gw = Q_GROUP * HEAD_DIM
    return pl.pallas_call(
        _attn_kernel,
        grid=(batch, N_KV_HEADS, nq),
        in_specs=[
            pl.BlockSpec((tq, gw), lambda b, h, i: (b * nq + i, h)),
            pl.BlockSpec((SEQ, HEAD_DIM), lambda b, h, i: (b, h)),
            pl.BlockSpec((SEQ, HEAD_DIM), lambda b, h, i: (b, N_KV_HEADS + h)),
        ],
        out_specs=pl.BlockSpec((tq, gw), lambda b, h, i: (b * nq + i, h)),
        out_shape=jax.ShapeDtypeStruct((m, ATTN_WIDTH), BF16),
        scratch_shapes=[pltpu.VMEM((SEQ, 2 * HEAD_DIM), BF16)],
        compiler_params=_params("arbitrary", "arbitrary", "arbitrary"),
        name="attention",
    )(q, kv, kv)


def _conv_kernel(u_ref, gb_ref, gc_ref, up_ref, gcp_ref, un_ref, gcn_ref, w_ref, o_ref, *, tr):
    i = pl.program_id(0)
    cu = gc_ref[...].astype(F32) * u_ref[...].astype(F32)
    starts_seq = (i * tr) % SEQ == 0
    ends_seq = ((i + 1) * tr) % SEQ == 0
    prev_row = gcp_ref[7:8, :].astype(F32) * up_ref[7:8, :].astype(F32)
    next_row = gcn_ref[0:1, :].astype(F32) * un_ref[0:1, :].astype(F32)
    prev_row = jnp.where(starts_seq, 0.0, prev_row)
    next_row = jnp.where(ends_seq, 0.0, next_row)
    row = lax.broadcasted_iota(jnp.int32, cu.shape, 0)
    cu_prev = jnp.where(row == 0, prev_row, pltpu.roll(cu, 1, axis=0))
    cu_next = jnp.where(row == tr - 1, next_row, pltpu.roll(cu, tr - 1, axis=0))
    y = w_ref[0, 0:1, :] * cu_prev + w_ref[0, 1:2, :] * cu + w_ref[0, 2:3, :] * cu_next
    o_ref[...] = (gb_ref[...].astype(F32) * y).astype(BF16)


def _gated_conv(proj, conv_w_all, layer):
    m = proj.shape[0]
    tr, tc = 512, 1024
    nrb = m // 8
    cu, cgb, cgc = REST_U // tc, REST_GB // tc, REST_GC // tc
    main = lambda off: pl.BlockSpec((tr, tc), lambda i, j: (i, off + j))
    prev = lambda off: pl.BlockSpec((8, tc), lambda i, j: (jnp.maximum(i * (tr // 8) - 1, 0), off + j))
    nxt = lambda off: pl.BlockSpec((8, tc), lambda i, j: (jnp.minimum((i + 1) * (tr // 8), nrb - 1), off + j))
    return pl.pallas_call(
        functools.partial(_conv_kernel, tr=tr),
        grid=(m // tr, CONV_WIDTH // tc),
        in_specs=[main(cu), main(cgb), main(cgc), prev(cu), prev(cgc), nxt(cu), nxt(cgc),
                  pl.BlockSpec((1, 3, tc), lambda i, j: (layer, 0, j))],
        out_specs=pl.BlockSpec((tr, tc), lambda i, j: (i, j)),
        out_shape=jax.ShapeDtypeStruct((m, CONV_WIDTH), BF16),
        compiler_params=_params("arbitrary", "arbitrary"),
        name="gated_conv",
    )(proj, proj, proj, proj, proj, proj, proj, conv_w_all)


MERGE_TM, MERGE_TN = 1024, 1024
V7X_VMEM_LIMIT_LARGE_BYTES = 62 * 1024 * 1024


def _merge_kernel(a_ref, c_ref, wa_hbm, wc_hbm, ga_ref, gv_ref, o_ref,
                  waf_ref, wab_ref, sema, wcf_ref, wcb_ref, semc, *, layer):
    _prefetched_weight_tile(wa_hbm, waf_ref, wab_ref, sema, layer=layer, tile0=0)
    _prefetched_weight_tile(wc_hbm, wcf_ref, wcb_ref, semc, layer=layer, tile0=0)
    ya = jnp.dot(a_ref[...], wab_ref[...], preferred_element_type=F32)
    yc = jnp.dot(c_ref[...], wcb_ref[...], preferred_element_type=F32)
    ga = jax.nn.sigmoid(ga_ref[...].astype(F32))
    gv = jax.nn.sigmoid(gv_ref[...].astype(F32))
    o_ref[...] = (ga * ya + gv * yc).astype(BF16)


def _merge(attn, conv, wa_all, wc_all, layer, proj):
    m = attn.shape[0]
    tm, tn = MERGE_TM, MERGE_TN
    return pl.pallas_call(
        functools.partial(_merge_kernel, layer=layer),
        grid=(D_MODEL // tn, m // tm),
        in_specs=[
            pl.BlockSpec((tm, ATTN_WIDTH), lambda j, i: (i, 0)),
            pl.BlockSpec((tm, CONV_WIDTH), lambda j, i: (i, 0)),
            pl.BlockSpec(memory_space=pl.ANY),
            pl.BlockSpec(memory_space=pl.ANY),
            pl.BlockSpec((tm, tn), lambda j, i: (i, REST_GA // tn + j)),
            pl.BlockSpec((tm, tn), lambda j, i: (i, REST_GV // tn + j)),
        ],
        out_specs=pl.BlockSpec((tm, tn), lambda j, i: (i, j)),
        out_shape=jax.ShapeDtypeStruct((m, D_MODEL), BF16),
        scratch_shapes=_weight_stream_scratch(ATTN_WIDTH, tn) + _weight_stream_scratch(CONV_WIDTH, tn),
        compiler_params=_params("arbitrary", "arbitrary", vmem_limit_bytes=V7X_VMEM_LIMIT_LARGE_BYTES),
        name="merge",
    )(attn, conv, wa_all, wc_all, proj, proj)


def _outproj_kernel(a_ref, w_hbm, x_ref, m_ref, o_ref, wf_ref, wb_ref, sem, *, layer, gate_row):
    _prefetched_weight_tile(w_hbm, wf_ref, wb_ref, sem, layer=layer, tile0=0)
    y = jnp.dot(a_ref[...], wb_ref[...], preferred_element_type=F32)
    o_ref[...] = x_ref[...] + (1.0 + m_ref[0, gate_row:gate_row + 1, :]) * y


def _outproj(a, w_all, layer, x, mod_l, gate_row):
    m, k = a.shape
    n = w_all.shape[2]
    tm, tn = 512, 1024
    return pl.pallas_call(
        functools.partial(_outproj_kernel, layer=layer, gate_row=gate_row),
        grid=(n // tn, m // tm),
        in_specs=[
            pl.BlockSpec((tm, k), lambda j, i: (i, 0)),
            pl.BlockSpec(memory_space=pl.ANY),
            pl.BlockSpec((tm, tn), lambda j, i: (i, j)),
            pl.BlockSpec((1, N_MOD, tn), lambda j, i: (i * tm // SEQ, 0, j)),
        ],
        out_specs=pl.BlockSpec((tm, tn), lambda j, i: (i, j)),
        out_shape=jax.ShapeDtypeStruct((m, n), F32),
        scratch_shapes=_weight_stream_scratch(k, tn),
        compiler_params=_params("arbitrary", "arbitrary"),
        name="outproj",
    )(a, w_all, x, mod_l)


def _mlp_in_kernel(h_ref, w_hbm, w2_ref, o_ref, w2b_ref, wf_ref, wb_ref, sem, *, layer):
    _prefetched_weight_tile(w_hbm, wf_ref, wb_ref, sem, layer=layer, tile0=0)
    w2b_ref[...] = w2_ref[0].astype(BF16)
    y = jnp.maximum(jnp.dot(h_ref[...], wb_ref[...], preferred_element_type=F32), 0.0)
    o_ref[...] = (y * y).astype(BF16)


def _mlp_in(h, w_all, w2_all, layer):
    m, d = h.shape
    n = w_all.shape[2]
    tm, tn = INPROJ_TM, INPROJ_TN
    nm = m // tm
    slab = w2_all.shape[1] // ((n // tn) * nm)
    assert slab * (n // tn) * nm == w2_all.shape[1] and slab % 16 == 0
    return pl.pallas_call(
        functools.partial(_mlp_in_kernel, layer=layer),
        grid=(n // tn, nm),
        in_specs=[
            pl.BlockSpec((tm, d), lambda j, i: (i, 0)),
            pl.BlockSpec(memory_space=pl.ANY),
            pl.BlockSpec((1, slab, w2_all.shape[2]), lambda j, i: (layer, j * nm + i, 0)),
        ],
        out_specs=[
            pl.BlockSpec((tm, tn), lambda j, i: (i, j)),
            pl.BlockSpec((slab, w2_all.shape[2]), lambda j, i: (j * nm + i, 0)),
        ],
        out_shape=[jax.ShapeDtypeStruct((m, n), BF16),
                   jax.ShapeDtypeStruct(w2_all.shape[1:], BF16)],
        scratch_shapes=_weight_stream_scratch(d, tn),
        compiler_params=_params("arbitrary", "arbitrary"),
        name="mlp_in",
    )(h, w_all, w2_all)


def _mlp_out_kernel(a_ref, w_ref, x_ref, m_ref, o_ref, *, gate_row):
    k = pl.program_id(2)
    last = pl.num_programs(2) - 1

    def partial_product():
        return jnp.dot(a_ref[...], w_ref[...], preferred_element_type=F32)

    @pl.when(k == 0)
    def _():
        o_ref[...] = partial_product()

    @pl.when((k > 0) & (k < last))
    def _():
        o_ref[...] += partial_product()

    @pl.when(k == last)
    def _():
        y = o_ref[...] + partial_product()
        o_ref[...] = x_ref[...] + (1.0 + m_ref[0, gate_row:gate_row + 1, :]) * y


def _mlp_out(a, w, x, mod_l, gate_row):
    m, kdim = a.shape
    n = w.shape[1]
    tm, tn, tk = 1024, 1024, 4096
    return pl.pallas_call(
        functools.partial(_mlp_out_kernel, gate_row=gate_row),
        grid=(m // tm, n // tn, kdim // tk),
        in_specs=[
            pl.BlockSpec((tm, tk), lambda i, j, k: (i, k)),
            pl.BlockSpec((tk, tn), lambda i, j, k: (k, j)),
            pl.BlockSpec((tm, tn), lambda i, j, k: (i, j)),
            pl.BlockSpec((1, N_MOD, tn), lambda i, j, k: (i * tm // SEQ, 0, j)),
        ],
        out_specs=pl.BlockSpec((tm, tn), lambda i, j, k: (i, j)),
        out_shape=jax.ShapeDtypeStruct((m, n), F32),
        compiler_params=_params("arbitrary", "arbitrary", "arbitrary"),
        name="mlp_out",
    )(a, w, x, mod_l)


def _rope_tables(seq):
    rows = seq // GRID_W
    row = jnp.repeat(jnp.arange(rows, dtype=F32), GRID_W)
    col = jnp.tile(jnp.arange(GRID_W, dtype=F32), rows)
    half = HEAD_DIM // 2
    freqs = ROPE_THETA ** (-jnp.arange(0, half, 2, dtype=F32) / half)
    ang_r = row[:, None] * freqs
    ang_c = col[:, None] * freqs
    zeros = jnp.zeros_like(ang_r)
    cos = jnp.concatenate([jnp.cos(ang_r), jnp.cos(ang_r), jnp.cos(ang_c), jnp.cos(ang_c)], axis=1)
    sin_lo = jnp.concatenate([-jnp.sin(ang_r), zeros, -jnp.sin(ang_c), zeros], axis=1)
    sin_hi = jnp.concatenate([zeros, jnp.sin(ang_r), zeros, jnp.sin(ang_c)], axis=1)
    return cos, sin_lo, sin_hi


def kernel(x, c, mod_w, mod_b, norm1_g, norm2_g, w_in, q_norm_g, k_norm_g, conv_w,
           w_branch_attn, w_branch_conv, w_out, w_mlp_in, w_mlp_out, final_norm_g):
    batch, seq, d = x.shape
    depth = mod_w.shape[0]
    assert (seq, d) == (SEQ, D_MODEL) and batch <= MOD_ROWS and w_in.shape[2] == IN_WIDTH
    m = batch * seq

    c_pad = jnp.zeros((MOD_ROWS, d), F32).at[:batch].set(c)
    mod = _modulation(c_pad, mod_w, mod_b)
    mod = mod[:, :batch].reshape(depth, batch, N_MOD, d)
    cos, sin_lo, sin_hi = _rope_tables(seq)

    xf = x.reshape(m, d)
    for l in range(depth):
        mod_l = mod[l]
        h = _norm_mod(xf, norm1_g[l], mod_l, sh_row=0, sc_row=1)
        tables = (cos, sin_lo, sin_hi)
        q = _inproj(h, w_in, l, 0, ATTN_WIDTH, q_norm_g[l], tables,
                    n_rope_heads=INPROJ_TN // HEAD_DIM, gain_scale=HEAD_DIM ** -0.5 * LOG2_E)
        kv = _inproj(h, w_in, l, COL_KV, 2 * KV_WIDTH, k_norm_g[l], tables, n_rope_heads=N_KV_HEADS)
        proj = _inproj(h, w_in, l, COL_REST, REST_WIDTH)
        attn = _attention(q, kv, batch)
        conv = _gated_conv(proj, conv_w, l)
        merged = _merge(attn, conv, w_branch_attn, w_branch_conv, l, proj)
        xf = _outproj(merged, w_out, l, xf, mod_l, gate_row=2)
        h = _norm_mod(xf, norm2_g[l], mod_l, sh_row=3, sc_row=4)
        a, w2b = _mlp_in(h, w_mlp_in, w_mlp_out, l)
        xf = _mlp_out(a, w2b, xf, mod_l, gate_row=5)
    return _final_norm(xf, final_norm_g).reshape(batch, seq, d)
```

```python
import functools
import math

import jax
import jax.numpy as jnp
from jax import lax
from jax.experimental import pallas as pl
from jax.experimental.pallas import tpu as pltpu

F32 = jnp.float32
BF16 = jnp.bfloat16

D_MODEL = 4096
SEQ = 4096
HEAD_DIM = 128
N_Q_HEADS = 16
N_KV_HEADS = 4
Q_GROUP = N_Q_HEADS // N_KV_HEADS
ATTN_WIDTH = N_Q_HEADS * HEAD_DIM
KV_WIDTH = N_KV_HEADS * HEAD_DIM
CONV_WIDTH = D_MODEL // 2
D_FF = 4 * D_MODEL
GRID_W = 64
ROPE_THETA = 10000.0
NORM_EPS = 1e-6
N_MOD = 6
LOG2_E = math.log2(math.e)

COL_KV = ATTN_WIDTH
COL_REST = COL_KV + 2 * KV_WIDTH
REST_U = 0
REST_GB = REST_U + CONV_WIDTH
REST_GC = REST_GB + CONV_WIDTH
REST_GA = REST_GC + CONV_WIDTH
REST_GV = REST_GA + D_MODEL
REST_WIDTH = REST_GV + D_MODEL
IN_WIDTH = COL_REST + REST_WIDTH

V7X_VMEM_LIMIT_BYTES = 58 * 1024 * 1024
MOD_ROWS = 16


def _params(*sem, vmem_limit_bytes=V7X_VMEM_LIMIT_BYTES):
    return pltpu.CompilerParams(dimension_semantics=sem, vmem_limit_bytes=vmem_limit_bytes)


def _mod_kernel(c_ref, w_ref, b_ref, o_ref):
    c = c_ref[...]
    a = (c * jax.nn.sigmoid(c)).astype(BF16)
    w = w_ref[0].astype(BF16)
    o_ref[0] = jnp.dot(a, w, preferred_element_type=F32) + b_ref[0]


def _modulation(c_pad, mod_w, mod_b):
    depth, d, n = mod_w.shape
    tn = 512
    return pl.pallas_call(
        _mod_kernel,
        grid=(depth, n // tn),
        in_specs=[
            pl.BlockSpec((MOD_ROWS, d), lambda l, j: (0, 0)),
            pl.BlockSpec((1, d, tn), lambda l, j: (l, 0, j)),
            pl.BlockSpec((1, 1, tn), lambda l, j: (l, 0, j)),
        ],
        out_specs=pl.BlockSpec((1, MOD_ROWS, tn), lambda l, j: (l, 0, j)),
        out_shape=jax.ShapeDtypeStruct((depth, MOD_ROWS, n), F32),
        compiler_params=_params("arbitrary", "arbitrary"),
        name="modulation",
    )(c_pad, mod_w, mod_b.reshape(depth, 1, n))


def _rms(x):
    return x * lax.rsqrt(jnp.mean(x * x, axis=-1, keepdims=True) + NORM_EPS)


def _norm_mod_kernel(x_ref, g_ref, m_ref, o_ref, *, sh_row, sc_row):
    y = _rms(x_ref[...]) * g_ref[...]
    scale = 1.0 + m_ref[0, sc_row:sc_row + 1, :]
    shift = m_ref[0, sh_row:sh_row + 1, :]
    o_ref[...] = (y * scale + shift).astype(BF16)


def _norm_mod(x, g, mod_l, sh_row, sc_row):
    m, d = x.shape
    tr = 512
    return pl.pallas_call(
        functools.partial(_norm_mod_kernel, sh_row=sh_row, sc_row=sc_row),
        grid=(m // tr,),
        in_specs=[
            pl.BlockSpec((tr, d), lambda i: (i, 0)),
            pl.BlockSpec((1, d), lambda i: (0, 0)),
            pl.BlockSpec((1, N_MOD, d), lambda i: (i * tr // SEQ, 0, 0)),
        ],
        out_specs=pl.BlockSpec((tr, d), lambda i: (i, 0)),
        out_shape=jax.ShapeDtypeStruct((m, d), BF16),
        compiler_params=_params("arbitrary"),
        name="norm_mod",
    )(x, g.reshape(1, d), mod_l)


def _final_norm_kernel(x_ref, g_ref, o_ref):
    o_ref[...] = _rms(x_ref[...]) * g_ref[...]


def _final_norm(x, g):
    m, d = x.shape
    tr = 512
    return pl.pallas_call(
        _final_norm_kernel,
        grid=(m // tr,),
        in_specs=[pl.BlockSpec((tr, d), lambda i: (i, 0)), pl.BlockSpec((1, d), lambda i: (0, 0))],
        out_specs=pl.BlockSpec((tr, d), lambda i: (i, 0)),
        out_shape=jax.ShapeDtypeStruct((m, d), F32),
        compiler_params=_params("arbitrary"),
        name="final_norm",
    )(x, g.reshape(1, d))


CAST_ROWS = 256


def _prefetched_weight_tile(w_hbm, wf_ref, wb_ref, sem, *, layer, tile0):
    j = pl.program_id(0)
    tn = wf_ref.shape[1]

    def copy(jj):
        cols = pl.ds(pl.multiple_of((tile0 + jj) * tn, tn), tn)
        return pltpu.make_async_copy(w_hbm.at[layer, :, cols], wf_ref, sem)

    @pl.when(pl.program_id(1) == 0)
    def _():
        @pl.when(j == 0)
        def _():
            copy(0).start()

        copy(j).wait()

        def body(r, carry):
            rows = pl.ds(pl.multiple_of(r * CAST_ROWS, CAST_ROWS), CAST_ROWS)
            wb_ref[rows, :] = wf_ref[rows, :].astype(BF16)
            return carry
        lax.fori_loop(0, wb_ref.shape[0] // CAST_ROWS, body, 0)

        @pl.when(j + 1 < pl.num_programs(0))
        def _():
            copy(j + 1).start()


def _weight_stream_scratch(k, tn):
    return [pltpu.VMEM((k, tn), F32), pltpu.VMEM((k, tn), BF16), pltpu.SemaphoreType.DMA(())]


def _head_norm_rope(x, g, cos, sin_lo, sin_hi):
    y = _rms(x) * g
    return (y * cos + pltpu.roll(y, HEAD_DIM - 32, axis=1) * sin_lo
            + pltpu.roll(y, 32, axis=1) * sin_hi)


INPROJ_TM, INPROJ_TN = 1024, 1024
MXU_WIDTH = 256
ROPE_ROW_CHUNK = 256


def _inproj_heads_kernel(h_ref, w_hbm, g_ref, cos_ref, slo_ref, shi_ref, o_ref, wf_ref, wb_ref, sem, *,
                         layer, tile0, n_rope_heads, gain_scale):
    _prefetched_weight_tile(w_hbm, wf_ref, wb_ref, sem, layer=layer, tile0=tile0)
    cos, slo, shi = cos_ref[...], slo_ref[...], shi_ref[...]
    g = g_ref[...] * gain_scale
    for c0 in range(0, o_ref.shape[1], MXU_WIDTH):
        for r0 in range(0, h_ref.shape[0], ROPE_ROW_CHUNK):
            rows = slice(r0, r0 + ROPE_ROW_CHUNK)
            acc = jnp.dot(h_ref[rows, :], wb_ref[:, c0:c0 + MXU_WIDTH], preferred_element_type=F32)
            for h in range(MXU_WIDTH // HEAD_DIM):
                sl = slice(c0 + h * HEAD_DIM, c0 + (h + 1) * HEAD_DIM)
                part = acc[:, h * HEAD_DIM:(h + 1) * HEAD_DIM]
                if c0 // HEAD_DIM + h < n_rope_heads:
                    o_ref[rows, sl] = _head_norm_rope(part, g, cos[rows], slo[rows], shi[rows]).astype(BF16)
                else:
                    o_ref[rows, sl] = part.astype(BF16)


def _inproj_plain_kernel(h_ref, w_hbm, o_ref, wf_ref, wb_ref, sem, *, layer, tile0):
    _prefetched_weight_tile(w_hbm, wf_ref, wb_ref, sem, layer=layer, tile0=tile0)
    o_ref[...] = jnp.dot(h_ref[...], wb_ref[...], preferred_element_type=F32).astype(BF16)


def _inproj(h, w_all, layer, col0, width, gain=None, tables=None, n_rope_heads=0, gain_scale=1.0):
    m, d = h.shape
    tm, tn = INPROJ_TM, INPROJ_TN
    assert col0 % tn == 0 and width % tn == 0 and SEQ % tm == 0
    in_specs = [
        pl.BlockSpec((tm, d), lambda j, i: (i, 0)),
        pl.BlockSpec(memory_space=pl.ANY),
    ]
    args = [h, w_all]
    if gain is None:
        body = functools.partial(_inproj_plain_kernel, layer=layer, tile0=col0 // tn)
    else:
        body = functools.partial(_inproj_heads_kernel, layer=layer, tile0=col0 // tn,
                                 n_rope_heads=n_rope_heads, gain_scale=gain_scale)
        tab = pl.BlockSpec((tm, HEAD_DIM), lambda j, i: (i % (SEQ // tm), 0))
        in_specs += [pl.BlockSpec((1, HEAD_DIM), lambda j, i: (0, 0)), tab, tab, tab]
        args += [gain.reshape(1, HEAD_DIM), *tables]
    return pl.pallas_call(
        body,
        grid=(width // tn, m // tm),
        in_specs=in_specs,
        out_specs=pl.BlockSpec((tm, tn), lambda j, i: (i, j)),
        out_shape=jax.ShapeDtypeStruct((m, width), BF16),
        scratch_shapes=_weight_stream_scratch(d, tn),
        compiler_params=_params("arbitrary", "arbitrary"),
        name="inproj_plain" if gain is None else f"inproj_rope{n_rope_heads}",
    )(*args)


def _attn_kernel(q_ref, k_ref, v_ref, o_ref, v1_ref):
    @pl.when(pl.program_id(2) == 0)
    def _():
        v1_ref[:, :HEAD_DIM] = v_ref[...]
        v1_ref[:, HEAD_DIM:] = jnp.ones((SEQ, HEAD_DIM), BF16)

    k = k_ref[...]
    v1 = v1_ref[...]
    heads = [slice(g * HEAD_DIM, (g + 1) * HEAD_DIM) for g in range(Q_GROUP)]
    scores = [lax.dot_general(q_ref[:, sl], k, (((1,), (1,)), ((), ())), preferred_element_type=F32)
              for sl in heads]
    for sl, s in zip(heads, scores):
        p = jnp.exp2(s - jnp.max(s, axis=-1, keepdims=True))
        o = jnp.dot(p.astype(BF16), v1, preferred_element_type=F32)
        o_ref[:, sl] = (o[:, :HEAD_DIM] / o[:, HEAD_DIM:HEAD_DIM + 1]).astype(BF16)


def _attention(q, kv, batch):
    m = q.shape[0]
    tq = 512
    nq = SEQ // tq
    gw = Q_GROUP * HEAD_DIM
    return pl.pallas_call(
        _attn_kernel,
        grid=(batch, N_KV_HEADS, nq),
        in_specs=[
            pl.BlockSpec((tq, gw), lambda b, h, i: (b * nq + i, h)),
            pl.BlockSpec((SEQ, HEAD_DIM), lambda b, h, i: (b, h)),
            pl.BlockSpec((SEQ, HEAD_DIM), lambda b, h, i: (b, N_KV_HEADS + h)),
        ],
        out_specs=pl.BlockSpec((tq, gw), lambda b, h, i: (b * nq + i, h)),
        out_shape=jax.ShapeDtypeStruct((m, ATTN_WIDTH), BF16),
        scratch_shapes=[pltpu.VMEM((SEQ, 2 * HEAD_DIM), BF16)],
        compiler_params=_params("arbitrary", "arbitrary", "arbitrary"),
        name="attention",
    )(q, kv, kv)


def _conv_kernel(u_ref, gb_ref, gc_ref, up_ref, gcp_ref, un_ref, gcn_ref, w_ref, o_ref, *, tr):
    i = pl.program_id(0)
    cu = gc_ref[...].astype(F32) * u_ref[...].astype(F32)
    starts_seq = (i * tr) % SEQ == 0
    ends_seq = ((i + 1) * tr) % SEQ == 0
    prev_row = gcp_ref[7:8, :].astype(F32) * up_ref[7:8, :].astype(F32)
    next_row = gcn_ref[0:1, :].astype(F32) * un_ref[0:1, :].astype(F32)
    prev_row = jnp.where(starts_seq, 0.0, prev_row)
    next_row = jnp.where(ends_seq, 0.0, next_row)
    row = lax.broadcasted_iota(jnp.int32, cu.shape, 0)
    cu_prev = jnp.where(row == 0, prev_row, pltpu.roll(cu, 1, axis=0))
    cu_next = jnp.where(row == tr - 1, next_row, pltpu.roll(cu, tr - 1, axis=0))
    y = w_ref[0, 0:1, :] * cu_prev + w_ref[0, 1:2, :] * cu + w_ref[0, 2:3, :] * cu_next
    o_ref[...] = (gb_ref[...].astype(F32) * y).astype(BF16)


def _gated_conv(proj, conv_w_all, layer):
    m = proj.shape[0]
    tr, tc = 512, 1024
    nrb = m // 8
    cu, cgb, cgc = REST_U // tc, REST_GB // tc, REST_GC // tc
    main = lambda off: pl.BlockSpec((tr, tc), lambda i, j: (i, off + j))
    prev = lambda off: pl.BlockSpec((8, tc), lambda i, j: (jnp.maximum(i * (tr // 8) - 1, 0), off + j))
    nxt = lambda off: pl.BlockSpec((8, tc), lambda i, j: (jnp.minimum((i + 1) * (tr // 8), nrb - 1), off + j))
    return pl.pallas_call(
        functools.partial(_conv_kernel, tr=tr),
        grid=(m // tr, CONV_WIDTH // tc),
        in_specs=[main(cu), main(cgb), main(cgc), prev(cu), prev(cgc), nxt(cu), nxt(cgc),
                  pl.BlockSpec((1, 3, tc), lambda i, j: (layer, 0, j))],
        out_specs=pl.BlockSpec((tr, tc), lambda i, j: (i, j)),
        out_shape=jax.ShapeDtypeStruct((m, CONV_WIDTH), BF16),
        compiler_params=_params("arbitrary", "arbitrary"),
        name="gated_conv",
    )(proj, proj, proj, proj, proj, proj, proj, conv_w_all)


MERGE_TM, MERGE_TN = 1024, 1024
V7X_VMEM_LIMIT_LARGE_BYTES = 62 * 1024 * 1024


def _merge_kernel(a_ref, c_ref, wa_hbm, wc_hbm, ga_ref, gv_ref, o_ref,
                  waf_ref, wab_ref, sema, wcf_ref, wcb_ref, semc, *, layer):
    _prefetched_weight_tile(wa_hbm, waf_ref, wab_ref, sema, layer=layer, tile0=0)
    _prefetched_weight_tile(wc_hbm, wcf_ref, wcb_ref, semc, layer=layer, tile0=0)
    ya = jnp.dot(a_ref[...], wab_ref[...], preferred_element_type=F32)
    yc = jnp.dot(c_ref[...], wcb_ref[...], preferred_element_type=F32)
    ga = jax.nn.sigmoid(ga_ref[...].astype(F32))
    gv = jax.nn.sigmoid(gv_ref[...].astype(F32))
    o_ref[...] = (ga * ya + gv * yc).astype(BF16)


def _merge(attn, conv, wa_all, wc_all, layer, proj):
    m = attn.shape[0]
    tm, tn = MERGE_TM, MERGE_TN
    return pl.pallas_call(
        functools.partial(_merge_kernel, layer=layer),
        grid=(D_MODEL // tn, m // tm),
        in_specs=[
            pl.BlockSpec((tm, ATTN_WIDTH), lambda j, i: (i, 0)),
            pl.BlockSpec((tm, CONV_WIDTH), lambda j, i: (i, 0)),
            pl.BlockSpec(memory_space=pl.ANY),
            pl.BlockSpec(memory_space=pl.ANY),
            pl.BlockSpec((tm, tn), lambda j, i: (i, REST_GA // tn + j)),
            pl.BlockSpec((tm, tn), lambda j, i: (i, REST_GV // tn + j)),
        ],
        out_specs=pl.BlockSpec((tm, tn), lambda j, i: (i, j)),
        out_shape=jax.ShapeDtypeStruct((m, D_MODEL), BF16),
        scratch_shapes=_weight_stream_scratch(ATTN_WIDTH, tn) + _weight_stream_scratch(CONV_WIDTH, tn),
        compiler_params=_params("arbitrary", "arbitrary", vmem_limit_bytes=V7X_VMEM_LIMIT_LARGE_BYTES),
        name="merge",
    )(attn, conv, wa_all, wc_all, proj, proj)


def _outproj_kernel(a_ref, w_hbm, x_ref, m_ref, o_ref, wf_ref, wb_ref, sem, *, layer, gate_row):
    _prefetched_weight_tile(w_hbm, wf_ref, wb_ref, sem, layer=layer, tile0=0)
    y = jnp.dot(a_ref[...], wb_ref[...], preferred_element_type=F32)
    o_ref[...] = x_ref[...] + (1.0 + m_ref[0, gate_row:gate_row + 1, :]) * y


def _outproj(a, w_all, layer, x, mod_l, gate_row):
    m, k = a.shape
    n = w_all.shape[2]
    tm, tn = 512, 1024
    return pl.pallas_call(
        functools.partial(_outproj_kernel, layer=layer, gate_row=gate_row),
        grid=(n // tn, m // tm),
        in_specs=[
            pl.BlockSpec((tm, k), lambda j, i: (i, 0)),
            pl.BlockSpec(memory_space=pl.ANY),
            pl.BlockSpec((tm, tn), lambda j, i: (i, j)),
            pl.BlockSpec((1, N_MOD, tn), lambda j, i: (i * tm // SEQ, 0, j)),
        ],
        out_specs=pl.BlockSpec((tm, tn), lambda j, i: (i, j)),
        out_shape=jax.ShapeDtypeStruct((m, n), F32),
        scratch_shapes=_weight_stream_scratch(k, tn),
        compiler_params=_params("arbitrary", "arbitrary"),
        name="outproj",
    )(a, w_all, x, mod_l)


def _mlp_in_kernel(h_ref, w_hbm, w2_ref, o_ref, w2b_ref, wf_ref, wb_ref, sem, *, layer):
    _prefetched_weight_tile(w_hbm, wf_ref, wb_ref, sem, layer=layer, tile0=0)
    w2b_ref[...] = w2_ref[0].astype(BF16)
    y = jnp.maximum(jnp.dot(h_ref[...], wb_ref[...], preferred_element_type=F32), 0.0)
    o_ref[...] = (y * y).astype(BF16)


def _mlp_in(h, w_all, w2_all, layer):
    m, d = h.shape
    n = w_all.shape[2]
    tm, tn = INPROJ_TM, INPROJ_TN
    nm = m // tm
    slab = w2_all.shape[1] // ((n // tn) * nm)
    assert slab * (n // tn) * nm == w2_all.shape[1] and slab % 16 == 0
    return pl.pallas_call(
        functools.partial(_mlp_in_kernel, layer=layer),
        grid=(n // tn, nm),
        in_specs=[
            pl.BlockSpec((tm, d), lambda j, i: (i, 0)),
            pl.BlockSpec(memory_space=pl.ANY),
            pl.BlockSpec((1, slab, w2_all.shape[2]), lambda j, i: (layer, j * nm + i, 0)),
        ],
        out_specs=[
            pl.BlockSpec((tm, tn), lambda j, i: (i, j)),
            pl.BlockSpec((slab, w2_all.shape[2]), lambda j, i: (j * nm + i, 0)),
        ],
        out_shape=[jax.ShapeDtypeStruct((m, n), BF16),
                   jax.ShapeDtypeStruct(w2_all.shape[1:], BF16)],
        scratch_shapes=_weight_stream_scratch(d, tn),
        compiler_params=_params("arbitrary", "arbitrary"),
        name="mlp_in",
    )(h, w_all, w2_all)


def _mlp_out_kernel(a_ref, w_ref, x_ref, m_ref, o_ref, *, gate_row):
    k = pl.program_id(2)
    last = pl.num_programs(2) - 1

    def partial_product():
        return jnp.dot(a_ref[...], w_ref[...], preferred_element_type=F32)

    @pl.when(k == 0)
    def _():
        o_ref[...] = partial_product()

    @pl.when((k > 0) & (k < last))
    def _():
        o_ref[...] += partial_product()

    @pl.when(k == last)
    def _():
        y = o_ref[...] + partial_product()
        o_ref[...] = x_ref[...] + (1.0 + m_ref[0, gate_row:gate_row + 1, :]) * y


def _mlp_out(a, w, x, mod_l, gate_row):
    m, kdim = a.shape
    n = w.shape[1]
    tm, tn, tk = 1024, 1024, 4096
    return pl.pallas_call(
        functools.partial(_mlp_out_kernel, gate_row=gate_row),
        grid=(m // tm, n // tn, kdim // tk),
        in_specs=[
            pl.BlockSpec((tm, tk), lambda i, j, k: (i, k)),
            pl.BlockSpec((tk, tn), lambda i, j, k: (k, j)),
            pl.BlockSpec((tm, tn), lambda i, j, k: (i, j)),
            pl.BlockSpec((1, N_MOD, tn), lambda i, j, k: (i * tm // SEQ, 0, j)),
        ],
        out_specs=pl.BlockSpec((tm, tn), lambda i, j, k: (i, j)),
        out_shape=jax.ShapeDtypeStruct((m, n), F32),
        compiler_params=_params("arbitrary", "arbitrary", "arbitrary"),
        name="mlp_out",
    )(a, w, x, mod_l)


def _rope_tables(seq):
    rows = seq // GRID_W
    row = jnp.repeat(jnp.arange(rows, dtype=F32), GRID_W)
    col = jnp.tile(jnp.arange(GRID_W, dtype=F32), rows)
    half = HEAD_DIM // 2
    freqs = ROPE_THETA ** (-jnp.arange(0, half, 2, dtype=F32) / half)
    ang_r = row[:, None] * freqs
    ang_c = col[:, None] * freqs
    zeros = jnp.zeros_like(ang_r)
    cos = jnp.concatenate([jnp.cos(ang_r), jnp.cos(ang_r), jnp.cos(ang_c), jnp.cos(ang_c)], axis=1)
    sin_lo = jnp.concatenate([-jnp.sin(ang_r), zeros, -jnp.sin(ang_c), zeros], axis=1)
    sin_hi = jnp.concatenate([zeros, jnp.sin(ang_r), zeros, jnp.sin(ang_c)], axis=1)
    return cos, sin_lo, sin_hi


def kernel(x, c, mod_w, mod_b, norm1_g, norm2_g, w_in, q_norm_g, k_norm_g, conv_w,
           w_branch_attn, w_branch_conv, w_out, w_mlp_in, w_mlp_out, final_norm_g):
    batch, seq, d = x.shape
    depth = mod_w.shape[0]
    assert (seq, d) == (SEQ, D_MODEL) and batch <= MOD_ROWS and w_in.shape[2] == IN_WIDTH
    m = batch * seq

    c_pad = jnp.zeros((MOD_ROWS, d), F32).at[:batch].set(c)
    mod = _modulation(c_pad, mod_w, mod_b)
    mod = mod[:, :batch].reshape(depth, batch, N_MOD, d)
    cos, sin_lo, sin_hi = _rope_tables(seq)

    xf = x.reshape(m, d)
    for l in range(depth):
        mod_l = mod[l]
        h = _norm_mod(xf, norm1_g[l], mod_l, sh_row=0, sc_row=1)
        tables = (cos, sin_lo, sin_hi)
        q = _inproj(h, w_in, l, 0, ATTN_WIDTH, q_norm_g[l], tables,
                    n_rope_heads=INPROJ_TN // HEAD_DIM, gain_scale=HEAD_DIM ** -0.5 * LOG2_E)
        kv = _inproj(h, w_in, l, COL_KV, 2 * KV_WIDTH, k_norm_g[l], tables, n_rope_heads=N_KV_HEADS)
        proj = _inproj(h, w_in, l, COL_REST, REST_WIDTH)
        attn = _attention(q, kv, batch)
        conv = _gated_conv(proj, conv_w, l)
        merged = _merge(attn, conv, w_branch_attn, w_branch_conv, l, proj)
        xf = _outproj(merged, w_out, l, xf, mod_l, gate_row=2)
        h = _norm_mod(xf, norm2_g[l], mod_l, sh_row=3, sc_row=4)
        a, w2b = _mlp_in(h, w_mlp_in, w_mlp_out, l)
        xf = _mlp_out(a, w2b, xf, mod_l, gate_row=5)
    return _final_norm(xf, final_norm_g).reshape(batch, seq, d)
```

```python
import functools
import math

import jax
import jax.numpy as jnp
from jax import lax
from jax.experimental import pallas as pl
from jax.experimental.pallas import tpu as pltpu

F32 = jnp.float32
BF16 = jnp.bfloat16

D_MODEL = 4096
SEQ = 4096
HEAD_DIM = 128
N_Q_HEADS = 16
N_KV_HEADS = 4
Q_GROUP = N_Q_HEADS // N_KV_HEADS
ATTN_WIDTH = N_Q_HEADS * HEAD_DIM
KV_WIDTH = N_KV_HEADS * HEAD_DIM
CONV_WIDTH = D_MODEL // 2
D_FF = 4 * D_MODEL
GRID_W = 64
ROPE_THETA = 10000.0
NORM_EPS = 1e-6
N_MOD = 6
LOG2_E = math.log2(math.e)

COL_KV = ATTN_WIDTH
COL_REST = COL_KV + 2 * KV_WIDTH
REST_U = 0
REST_GB = REST_U + CONV_WIDTH
REST_GC = REST_GB + CONV_WIDTH
REST_GA = REST_GC + CONV_WIDTH
REST_GV = REST_GA + D_MODEL
REST_WIDTH = REST_GV + D_MODEL
IN_WIDTH = COL_REST + REST_WIDTH

V7X_VMEM_LIMIT_BYTES = 58 * 1024 * 1024
MOD_ROWS = 16


def _params(*sem, vmem_limit_bytes=V7X_VMEM_LIMIT_BYTES):
    return pltpu.CompilerParams(dimension_semantics=sem, vmem_limit_bytes=vmem_limit_bytes)


def _mod_kernel(c_ref, w_ref, b_ref, o_ref):
    c = c_ref[...]
    a = (c * jax.nn.sigmoid(c)).astype(BF16)
    w = w_ref[0].astype(BF16)
    o_ref[0] = jnp.dot(a, w, preferred_element_type=F32) + b_ref[0]


def _modulation(c_pad, mod_w, mod_b):
    depth, d, n = mod_w.shape
    tn = 512
    return pl.pallas_call(
        _mod_kernel,
        grid=(depth, n // tn),
        in_specs=[
            pl.BlockSpec((MOD_ROWS, d), lambda l, j: (0, 0)),
            pl.BlockSpec((1, d, tn), lambda l, j: (l, 0, j)),
            pl.BlockSpec((1, 1, tn), lambda l, j: (l, 0, j)),
        ],
        out_specs=pl.BlockSpec((1, MOD_ROWS, tn), lambda l, j: (l, 0, j)),
        out_shape=jax.ShapeDtypeStruct((depth, MOD_ROWS, n), F32),
        compiler_params=_params("arbitrary", "arbitrary"),
        name="modulation",
    )(c_pad, mod_w, mod_b.reshape(depth, 1, n))


def _rms(x):
    return x * lax.rsqrt(jnp.mean(x * x, axis=-1, keepdims=True) + NORM_EPS)


def _norm_mod_kernel(x_ref, g_ref, m_ref, o_ref, *, sh_row, sc_row):
    y = _rms(x_ref[...]) * g_ref[...]
    scale = 1.0 + m_ref[0, sc_row:sc_row + 1, :]
    shift = m_ref[0, sh_row:sh_row + 1, :]
    o_ref[...] = (y * scale + shift).astype(BF16)


def _norm_mod(x, g, mod_l, sh_row, sc_row):
    m, d = x.shape
    tr = 512
    return pl.pallas_call(
        functools.partial(_norm_mod_kernel, sh_row=sh_row, sc_row=sc_row),
        grid=(m // tr,),
        in_specs=[
            pl.BlockSpec((tr, d), lambda i: (i, 0)),
            pl.BlockSpec((1, d), lambda i: (0, 0)),
            pl.BlockSpec((1, N_MOD, d), lambda i: (i * tr // SEQ, 0, 0)),
        ],
        out_specs=pl.BlockSpec((tr, d), lambda i: (i, 0)),
        out_shape=jax.ShapeDtypeStruct((m, d), BF16),
        compiler_params=_params("arbitrary"),
        name="norm_mod",
    )(x, g.reshape(1, d), mod_l)


def _final_norm_kernel(x_ref, g_ref, o_ref):
    o_ref[...] = _rms(x_ref[...]) * g_ref[...]


def _final_norm(x, g):
    m, d = x.shape
    tr = 512
    return pl.pallas_call(
        _final_norm_kernel,
        grid=(m // tr,),
        in_specs=[pl.BlockSpec((tr, d), lambda i: (i, 0)), pl.BlockSpec((1, d), lambda i: (0, 0))],
        out_specs=pl.BlockSpec((tr, d), lambda i: (i, 0)),
        out_shape=jax.ShapeDtypeStruct((m, d), F32),
        compiler_params=_params("arbitrary"),
        name="final_norm",
    )(x, g.reshape(1, d))


CAST_ROWS = 256


def _prefetched_weight_tile(w_hbm, wf_ref, wb_ref, sem, *, layer, tile0):
    j = pl.program_id(0)
    tn = wf_ref.shape[1]

    def copy(jj):
        cols = pl.ds(pl.multiple_of((tile0 + jj) * tn, tn), tn)
        return pltpu.make_async_copy(w_hbm.at[layer, :, cols], wf_ref, sem)

    @pl.when(pl.program_id(1) == 0)
    def _():
        @pl.when(j == 0)
        def _():
            copy(0).start()

        copy(j).wait()

        def body(r, carry):
            rows = pl.ds(pl.multiple_of(r * CAST_ROWS, CAST_ROWS), CAST_ROWS)
            wb_ref[rows, :] = wf_ref[rows, :].astype(BF16)
            return carry
        lax.fori_loop(0, wb_ref.shape[0] // CAST_ROWS, body, 0)

        @pl.when(j + 1 < pl.num_programs(0))
        def _():
            copy(j + 1).start()


def _weight_stream_scratch(k, tn):
    return [pltpu.VMEM((k, tn), F32), pltpu.VMEM((k, tn), BF16), pltpu.SemaphoreType.DMA(())]


def _head_norm_rope(x, g, cos, sin_lo, sin_hi):
    y = _rms(x) * g
    return (y * cos + pltpu.roll(y, HEAD_DIM - 32, axis=1) * sin_lo
            + pltpu.roll(y, 32, axis=1) * sin_hi)


INPROJ_TM, INPROJ_TN = 1024, 1024
MXU_WIDTH = 256
ROPE_ROW_CHUNK = 256
DOT_ROW_CHUNK = 256


def _inproj_heads_kernel(h_ref, w_hbm, g_ref, cos_ref, slo_ref, shi_ref, o_ref, wf_ref, wb_ref, sem, *,
                         layer, tile0, n_rope_heads, gain_scale):
    _prefetched_weight_tile(w_hbm, wf_ref, wb_ref, sem, layer=layer, tile0=tile0)
    cos, slo, shi = cos_ref[...], slo_ref[...], shi_ref[...]
    g = g_ref[...] * gain_scale
    for c0 in range(0, o_ref.shape[1], MXU_WIDTH):
        for r0 in range(0, h_ref.shape[0], ROPE_ROW_CHUNK):
            rows = slice(r0, r0 + ROPE_ROW_CHUNK)
            acc = jnp.dot(h_ref[rows, :], wb_ref[:, c0:c0 + MXU_WIDTH], preferred_element_type=F32)
            for h in range(MXU_WIDTH // HEAD_DIM):
                sl = slice(c0 + h * HEAD_DIM, c0 + (h + 1) * HEAD_DIM)
                part = acc[:, h * HEAD_DIM:(h + 1) * HEAD_DIM]
                if c0 // HEAD_DIM + h < n_rope_heads:
                    o_ref[rows, sl] = _head_norm_rope(part, g, cos[rows], slo[rows], shi[rows]).astype(BF16)
                else:
                    o_ref[rows, sl] = part.astype(BF16)


def _inproj_plain_kernel(h_ref, w_hbm, o_ref, wf_ref, wb_ref, sem, *, layer, tile0):
    _prefetched_weight_tile(w_hbm, wf_ref, wb_ref, sem, layer=layer, tile0=tile0)
    o_ref[...] = jnp.dot(h_ref[...], wb_ref[...], preferred_element_type=F32).astype(BF16)


def _inproj(h, w_all, layer, col0, width, gain=None, tables=None, n_rope_heads=0, gain_scale=1.0):
    m, d = h.shape
    tm, tn = INPROJ_TM, INPROJ_TN
    assert col0 % tn == 0 and width % tn == 0 and SEQ % tm == 0
    in_specs = [
        pl.BlockSpec((tm, d), lambda j, i: (i, 0)),
        pl.BlockSpec(memory_space=pl.ANY),
    ]
    args = [h, w_all]
    if gain is None:
        body = functools.partial(_inproj_plain_kernel, layer=layer, tile0=col0 // tn)
    else:
        body = functools.partial(_inproj_heads_kernel, layer=layer, tile0=col0 // tn,
                                 n_rope_heads=n_rope_heads, gain_scale=gain_scale)
        tab = pl.BlockSpec((tm, HEAD_DIM), lambda j, i: (i % (SEQ // tm), 0))
        in_specs += [pl.BlockSpec((1, HEAD_DIM), lambda j, i: (0, 0)), tab, tab, tab]
        args += [gain.reshape(1, HEAD_DIM), *tables]
    return pl.pallas_call(
        body,
        grid=(width // tn, m // tm),
        in_specs=in_specs,
        out_specs=pl.BlockSpec((tm, tn), lambda j, i: (i, j)),
        out_shape=jax.ShapeDtypeStruct((m, width), BF16),
        scratch_shapes=_weight_stream_scratch(d, tn),
        compiler_params=_params("arbitrary", "arbitrary"),
        name="inproj_plain" if gain is None else f"inproj_rope{n_rope_heads}",
    )(*args)


def _attn_kernel(q_ref, k_ref, v_ref, o_ref, v1_ref):
    @pl.when(pl.program_id(2) == 0)
    def _():
        v1_ref[:, :HEAD_DIM] = v_ref[...]
        v1_ref[:, HEAD_DIM:] = jnp.ones((SEQ, HEAD_DIM), BF16)

    k = k_ref[...]
    v1 = v1_ref[...]
    heads = [slice(g * HEAD_DIM, (g + 1) * HEAD_DIM) for g in range(Q_GROUP)]
    scores = [lax.dot_general(q_ref[:, sl], k, (((1,), (1,)), ((), ())), preferred_element_type=F32)
              for sl in heads]
    for sl, s in zip(heads, scores):
        p = jnp.exp2(s - jnp.max(s, axis=-1, keepdims=True))
        o = jnp.dot(p.astype(BF16), v1, preferred_element_type=F32)
        o_ref[:, sl] = (o[:, :HEAD_DIM] / o[:, HEAD_DIM:HEAD_DIM + 1]).astype(BF16)


def _attention(q, kv, batch):
    m = q.shape[0]
    tq = 512
    nq = SEQ // tq
    gw = Q_GROUP * HEAD_DIM
    return pl.pallas_call(
        _attn_kernel,
        grid=(batch, N_KV_HEADS, nq),
        in_specs=[
            pl.BlockSpec((tq, gw), lambda b, h, i: (b * nq + i, h)),
            pl.BlockSpec((SEQ, HEAD_DIM), lambda b, h, i: (b, h)),
            pl.BlockSpec((SEQ, HEAD_DIM), lambda b, h, i: (b, N_KV_HEADS + h)),
        ],
        out_specs=pl.BlockSpec((tq, gw), lambda b, h, i: (b * nq + i, h)),
        out_shape=jax.ShapeDtypeStruct((m, ATTN_WIDTH), BF16),
        scratch_shapes=[pltpu.VMEM((SEQ, 2 * HEAD_DIM), BF16)],
        compiler_params=_params("arbitrary", "arbitrary", "arbitrary"),
        name="attention",
    )(q, kv, kv)


def _conv_kernel(u_ref, gb_ref, gc_ref, up_ref, gcp_ref, un_ref, gcn_ref, w_ref, o_ref, *, tr):
    i = pl.program_id(0)
    cu = gc_ref[...].astype(F32) * u_ref[...].astype(F32)
    starts_seq = (i * tr) % SEQ == 0
    ends_seq = ((i + 1) * tr) % SEQ == 0
    prev_row = gcp_ref[7:8, :].astype(F32) * up_ref[7:8, :].astype(F32)
    next_row = gcn_ref[0:1, :].astype(F32) * un_ref[0:1, :].astype(F32)
    prev_row = jnp.where(starts_seq, 0.0, prev_row)
    next_row = jnp.where(ends_seq, 0.0, next_row)
    row = lax.broadcasted_iota(jnp.int32, cu.shape, 0)
    cu_prev = jnp.where(row == 0, prev_row, pltpu.roll(cu, 1, axis=0))
    cu_next = jnp.where(row == tr - 1, next_row, pltpu.roll(cu, tr - 1, axis=0))
    y = w_ref[0, 0:1, :] * cu_prev + w_ref[0, 1:2, :] * cu + w_ref[0, 2:3, :] * cu_next
    o_ref[...] = (gb_ref[...].astype(F32) * y).astype(BF16)


def _gated_conv(proj, conv_w_all, layer):
    m = proj.shape[0]
    tr, tc = 1024, 1024
    nrb = m // 8
    cu, cgb, cgc = REST_U // tc, REST_GB // tc, REST_GC // tc
    main = lambda off: pl.BlockSpec((tr, tc), lambda i, j: (i, off + j))
    prev = lambda off: pl.BlockSpec((8, tc), lambda i, j: (jnp.maximum(i * (tr // 8) - 1, 0), off + j))
    nxt = lambda off: pl.BlockSpec((8, tc), lambda i, j: (jnp.minimum((i + 1) * (tr // 8), nrb - 1), off + j))
    return pl.pallas_call(
        functools.partial(_conv_kernel, tr=tr),
        grid=(m // tr, CONV_WIDTH // tc),
        in_specs=[main(cu), main(cgb), main(cgc), prev(cu), prev(cgc), nxt(cu), nxt(cgc),
                  pl.BlockSpec((1, 3, tc), lambda i, j: (layer, 0, j))],
        out_specs=pl.BlockSpec((tr, tc), lambda i, j: (i, j)),
        out_shape=jax.ShapeDtypeStruct((m, CONV_WIDTH), BF16),
        compiler_params=_params("arbitrary", "arbitrary"),
        name="gated_conv",
    )(proj, proj, proj, proj, proj, proj, proj, conv_w_all)


MERGE_TM, MERGE_TN = 1024, 1024


def _merge_kernel(a_ref, c_ref, wa_hbm, wc_hbm, ga_ref, gv_ref, o_ref,
                  waf_ref, wab_ref, sema, wcf_ref, wcb_ref, semc, *, layer):
    _prefetched_weight_tile(wa_hbm, waf_ref, wab_ref, sema, layer=layer, tile0=0)
    _prefetched_weight_tile(wc_hbm, wcf_ref, wcb_ref, semc, layer=layer, tile0=0)
    for r0 in range(0, a_ref.shape[0], DOT_ROW_CHUNK):
        rows = slice(r0, r0 + DOT_ROW_CHUNK)
        ya = jnp.dot(a_ref[rows, :], wab_ref[...], preferred_element_type=F32)
        yc = jnp.dot(c_ref[rows, :], wcb_ref[...], preferred_element_type=F32)
        ga = jax.nn.sigmoid(ga_ref[rows, :].astype(F32))
        gv = jax.nn.sigmoid(gv_ref[rows, :].astype(F32))
        o_ref[rows, :] = (ga * ya + gv * yc).astype(BF16)


def _merge(attn, conv, wa_all, wc_all, layer, proj):
    m = attn.shape[0]
    tm, tn = MERGE_TM, MERGE_TN
    return pl.pallas_call(
        functools.partial(_merge_kernel, layer=layer),
        grid=(D_MODEL // tn, m // tm),
        in_specs=[
            pl.BlockSpec((tm, ATTN_WIDTH), lambda j, i: (i, 0)),
            pl.BlockSpec((tm, CONV_WIDTH), lambda j, i: (i, 0)),
            pl.BlockSpec(memory_space=pl.ANY),
            pl.BlockSpec(memory_space=pl.ANY),
            pl.BlockSpec((tm, tn), lambda j, i: (i, REST_GA // tn + j)),
            pl.BlockSpec((tm, tn), lambda j, i: (i, REST_GV // tn + j)),
        ],
        out_specs=pl.BlockSpec((tm, tn), lambda j, i: (i, j)),
        out_shape=jax.ShapeDtypeStruct((m, D_MODEL), BF16),
        scratch_shapes=_weight_stream_scratch(ATTN_WIDTH, tn) + _weight_stream_scratch(CONV_WIDTH, tn),
        compiler_params=_params("arbitrary", "arbitrary"),
        name="merge",
    )(attn, conv, wa_all, wc_all, proj, proj)


def _outproj_kernel(a_ref, w_hbm, x_ref, m_ref, o_ref, wf_ref, wb_ref, sem, *, layer, gate_row):
    _prefetched_weight_tile(w_hbm, wf_ref, wb_ref, sem, layer=layer, tile0=0)
    gate = 1.0 + m_ref[0, gate_row:gate_row + 1, :]
    for r0 in range(0, a_ref.shape[0], DOT_ROW_CHUNK):
        rows = slice(r0, r0 + DOT_ROW_CHUNK)
        y = jnp.dot(a_ref[rows, :], wb_ref[...], preferred_element_type=F32)
        o_ref[rows, :] = x_ref[rows, :] + gate * y


def _outproj(a, w_all, layer, x, mod_l, gate_row):
    m, k = a.shape
    n = w_all.shape[2]
    tm, tn = MERGE_TM, MERGE_TN
    return pl.pallas_call(
        functools.partial(_outproj_kernel, layer=layer, gate_row=gate_row),
        grid=(n // tn, m // tm),
        in_specs=[
            pl.BlockSpec((tm, k), lambda j, i: (i, 0)),
            pl.BlockSpec(memory_space=pl.ANY),
            pl.BlockSpec((tm, tn), lambda j, i: (i, j)),
            pl.BlockSpec((1, N_MOD, tn), lambda j, i: (i * tm // SEQ, 0, j)),
        ],
        out_specs=pl.BlockSpec((tm, tn), lambda j, i: (i, j)),
        out_shape=jax.ShapeDtypeStruct((m, n), F32),
        scratch_shapes=_weight_stream_scratch(k, tn),
        compiler_params=_params("arbitrary", "arbitrary"),
        name="outproj",
    )(a, w_all, x, mod_l)


def _mlp_in_kernel(h_ref, w_hbm, w2_ref, o_ref, w2b_ref, wf_ref, wb_ref, sem, *, layer):
    _prefetched_weight_tile(w_hbm, wf_ref, wb_ref, sem, layer=layer, tile0=0)
    w2b_ref[...] = w2_ref[0].astype(BF16)
    y = jnp.maximum(jnp.dot(h_ref[...], wb_ref[...], preferred_element_type=F32), 0.0)
    o_ref[...] = (y * y).astype(BF16)


def _mlp_in(h, w_all, w2_all, layer):
    m, d = h.shape
    n = w_all.shape[2]
    tm, tn = INPROJ_TM, INPROJ_TN
    nm = m // tm
    slab = w2_all.shape[1] // ((n // tn) * nm)
    assert slab * (n // tn) * nm == w2_all.shape[1] and slab % 16 == 0
    return pl.pallas_call(
        functools.partial(_mlp_in_kernel, layer=layer),
        grid=(n // tn, nm),
        in_specs=[
            pl.BlockSpec((tm, d), lambda j, i: (i, 0)),
            pl.BlockSpec(memory_space=pl.ANY),
            pl.BlockSpec((1, slab, w2_all.shape[2]), lambda j, i: (layer, j * nm + i, 0)),
        ],
        out_specs=[
            pl.BlockSpec((tm, tn), lambda j, i: (i, j)),
            pl.BlockSpec((slab, w2_all.shape[2]), lambda j, i: (j * nm + i, 0)),
        ],
        out_shape=[jax.ShapeDtypeStruct((m, n), BF16),
                   jax.ShapeDtypeStruct(w2_all.shape[1:], BF16)],
        scratch_shapes=_weight_stream_scratch(d, tn),
        compiler_params=_params("arbitrary", "arbitrary"),
        name="mlp_in",
    )(h, w_all, w2_all)


def _mlp_out_kernel(a_ref, w_ref, x_ref, m_ref, o_ref, *, gate_row):
    k = pl.program_id(2)
    last = pl.num_programs(2) - 1

    def partial_product():
        return jnp.dot(a_ref[...], w_ref[...], preferred_element_type=F32)

    @pl.when(k == 0)
    def _():
        o_ref[...] = partial_product()

    @pl.when((k > 0) & (k < last))
    def _():
        o_ref[...] += partial_product()

    @pl.when(k == last)
    def _():
        y = o_ref[...] + partial_product()
        o_ref[...] = x_ref[...] + (1.0 + m_ref[0, gate_row:gate_row + 1, :]) * y


def _mlp_out(a, w, x, mod_l, gate_row):
    m, kdim = a.shape
    n = w.shape[1]
    tm, tn, tk = 1024, 1024, 4096
    return pl.pallas_call(
        functools.partial(_mlp_out_kernel, gate_row=gate_row),
        grid=(m // tm, n // tn, kdim // tk),
        in_specs=[
            pl.BlockSpec((tm, tk), lambda i, j, k: (i, k)),
            pl.BlockSpec((tk, tn), lambda i, j, k: (k, j)),
            pl.BlockSpec((tm, tn), lambda i, j, k: (i, j)),
            pl.BlockSpec((1, N_MOD, tn), lambda i, j, k: (i * tm // SEQ, 0, j)),
        ],
        out_specs=pl.BlockSpec((tm, tn), lambda i, j, k: (i, j)),
        out_shape=jax.ShapeDtypeStruct((m, n), F32),
        compiler_params=_params("arbitrary", "arbitrary", "arbitrary"),
        name="mlp_out",
    )(a, w, x, mod_l)


def _rope_tables(seq):
    rows = seq // GRID_W
    row = jnp.repeat(jnp.arange(rows, dtype=F32), GRID_W)
    col = jnp.tile(jnp.arange(GRID_W, dtype=F32), rows)
    half = HEAD_DIM // 2
    freqs = ROPE_THETA ** (-jnp.arange(0, half, 2, dtype=F32) / half)
    ang_r = row[:, None] * freqs
    ang_c = col[:, None] * freqs
    zeros = jnp.zeros_like(ang_r)
    cos = jnp.concatenate([jnp.cos(ang_r), jnp.cos(ang_r), jnp.cos(ang_c), jnp.cos(ang_c)], axis=1)
    sin_lo = jnp.concatenate([-jnp.sin(ang_r), zeros, -jnp.sin(ang_c), zeros], axis=1)
    sin_hi = jnp.concatenate([zeros, jnp.sin(ang_r), zeros, jnp.sin(ang_c)], axis=1)
    return cos, sin_lo, sin_hi


def kernel(x, c, mod_w, mod_b, norm1_g, norm2_g, w_in, q_norm_g, k_norm_g, conv_w,
           w_branch_attn, w_branch_conv, w_out, w_mlp_in, w_mlp_out, final_norm_g):
    batch, seq, d = x.shape
    depth = mod_w.shape[0]
    assert (seq, d) == (SEQ, D_MODEL) and batch <= MOD_ROWS and w_in.shape[2] == IN_WIDTH
    m = batch * seq

    c_pad = jnp.zeros((MOD_ROWS, d), F32).at[:batch].set(c)
    mod = _modulation(c_pad, mod_w, mod_b)
    mod = mod[:, :batch].reshape(depth, batch, N_MOD, d)
    cos, sin_lo, sin_hi = _rope_tables(seq)

    xf = x.reshape(m, d)
    for l in range(depth):
        mod_l = mod[l]
        h = _norm_mod(xf, norm1_g[l], mod_l, sh_row=0, sc_row=1)
        tables = (cos, sin_lo, sin_hi)
        q = _inproj(h, w_in, l, 0, ATTN_WIDTH, q_norm_g[l], tables,
                    n_rope_heads=INPROJ_TN // HEAD_DIM, gain_scale=HEAD_DIM ** -0.5 * LOG2_E)
        kv = _inproj(h, w_in, l, COL_KV, 2 * KV_WIDTH, k_norm_g[l], tables, n_rope_heads=N_KV_HEADS)
        proj = _inproj(h, w_in, l, COL_REST, REST_WIDTH)
        attn = _attention(q, kv, batch)
        conv = _gated_conv(proj, conv_w, l)
        merged = _merge(attn, conv, w_branch_attn, w_branch_conv, l, proj)
        xf = _outproj(merged, w_out, l, xf, mod_l, gate_row=2)
        h = _norm_mod(xf, norm2_g[l], mod_l, sh_row=3, sc_row=4)
        a, w2b = _mlp_in(h, w_mlp_in, w_mlp_out, l)
        xf = _mlp_out(a, w2b, xf, mod_l, gate_row=5)
    return _final_norm(xf, final_norm_g).reshape(batch, seq, d)
```

```python
import functools
import math

import jax
import jax.numpy as jnp
from jax import lax
from jax.experimental import pallas as pl
from jax.experimental.pallas import tpu as pltpu

F32 = jnp.float32
BF16 = jnp.bfloat16

D_MODEL = 4096
SEQ = 4096
HEAD_DIM = 128
N_Q_HEADS = 16
N_KV_HEADS = 4
Q_GROUP = N_Q_HEADS // N_KV_HEADS
ATTN_WIDTH = N_Q_HEADS * HEAD_DIM
KV_WIDTH = N_KV_HEADS * HEAD_DIM
CONV_WIDTH = D_MODEL // 2
D_FF = 4 * D_MODEL
GRID_W = 64
ROPE_THETA = 10000.0
NORM_EPS = 1e-6
N_MOD = 6
LOG2_E = math.log2(math.e)

COL_KV = ATTN_WIDTH
COL_REST = COL_KV + 2 * KV_WIDTH
REST_U = 0
REST_GB = REST_U + CONV_WIDTH
REST_GC = REST_GB + CONV_WIDTH
REST_GA = REST_GC + CONV_WIDTH
REST_GV = REST_GA + D_MODEL
REST_WIDTH = REST_GV + D_MODEL
IN_WIDTH = COL_REST + REST_WIDTH

V7X_VMEM_LIMIT_BYTES = 58 * 1024 * 1024
MOD_ROWS = 16


def _params(*sem, vmem_limit_bytes=V7X_VMEM_LIMIT_BYTES):
    return pltpu.CompilerParams(dimension_semantics=sem, vmem_limit_bytes=vmem_limit_bytes)


def _mod_kernel(c_ref, w_ref, b_ref, o_ref):
    c = c_ref[...]
    a = (c * jax.nn.sigmoid(c)).astype(BF16)
    w = w_ref[0].astype(BF16)
    o_ref[0] = jnp.dot(a, w, preferred_element_type=F32) + b_ref[0]


def _modulation(c_pad, mod_w, mod_b):
    depth, d, n = mod_w.shape
    tn = 512
    return pl.pallas_call(
        _mod_kernel,
        grid=(depth, n // tn),
        in_specs=[
            pl.BlockSpec((MOD_ROWS, d), lambda l, j: (0, 0)),
            pl.BlockSpec((1, d, tn), lambda l, j: (l, 0, j)),
            pl.BlockSpec((1, 1, tn), lambda l, j: (l, 0, j)),
        ],
        out_specs=pl.BlockSpec((1, MOD_ROWS, tn), lambda l, j: (l, 0, j)),
        out_shape=jax.ShapeDtypeStruct((depth, MOD_ROWS, n), F32),
        compiler_params=_params("arbitrary", "arbitrary"),
        name="modulation",
    )(c_pad, mod_w, mod_b.reshape(depth, 1, n))


def _rms(x):
    return x * lax.rsqrt(jnp.mean(x * x, axis=-1, keepdims=True) + NORM_EPS)


NORM_ROW_CHUNK = 16


def _norm_mod_kernel(x_ref, g_ref, m_ref, o_ref, *, sh_row, sc_row):
    gain = g_ref[...]
    scale = 1.0 + m_ref[0, sc_row:sc_row + 1, :]
    shift = m_ref[0, sh_row:sh_row + 1, :]

    def body(c, carry):
        rows = pl.ds(pl.multiple_of(c * NORM_ROW_CHUNK, NORM_ROW_CHUNK), NORM_ROW_CHUNK)
        y = _rms(x_ref[rows, :]) * gain
        o_ref[rows, :] = (y * scale + shift).astype(BF16)
        return carry
    lax.fori_loop(0, x_ref.shape[0] // NORM_ROW_CHUNK, body, 0, unroll=4)


def _norm_mod(x, g, mod_l, sh_row, sc_row):
    m, d = x.shape
    tr = 1024
    return pl.pallas_call(
        functools.partial(_norm_mod_kernel, sh_row=sh_row, sc_row=sc_row),
        grid=(m // tr,),
        in_specs=[
            pl.BlockSpec((tr, d), lambda i: (i, 0)),
            pl.BlockSpec((1, d), lambda i: (0, 0)),
            pl.BlockSpec((1, N_MOD, d), lambda i: (i * tr // SEQ, 0, 0)),
        ],
        out_specs=pl.BlockSpec((tr, d), lambda i: (i, 0)),
        out_shape=jax.ShapeDtypeStruct((m, d), BF16),
        compiler_params=_params("arbitrary"),
        name="norm_mod",
    )(x, g.reshape(1, d), mod_l)


def _final_norm_kernel(x_ref, g_ref, o_ref):
    gain = g_ref[...]

    def body(c, carry):
        rows = pl.ds(pl.multiple_of(c * NORM_ROW_CHUNK, NORM_ROW_CHUNK), NORM_ROW_CHUNK)
        o_ref[rows, :] = _rms(x_ref[rows, :]) * gain
        return carry
    lax.fori_loop(0, x_ref.shape[0] // NORM_ROW_CHUNK, body, 0, unroll=4)


def _final_norm(x, g):
    m, d = x.shape
    tr = 512
    return pl.pallas_call(
        _final_norm_kernel,
        grid=(m // tr,),
        in_specs=[pl.BlockSpec((tr, d), lambda i: (i, 0)), pl.BlockSpec((1, d), lambda i: (0, 0))],
        out_specs=pl.BlockSpec((tr, d), lambda i: (i, 0)),
        out_shape=jax.ShapeDtypeStruct((m, d), F32),
        compiler_params=_params("arbitrary"),
        name="final_norm",
    )(x, g.reshape(1, d))


CAST_ROWS = 256


def _prefetched_weight_tile(w_hbm, wf_ref, wb_ref, sem, *, layer, tile0):
    j = pl.program_id(0)
    tn = wf_ref.shape[1]

    def copy(jj):
        cols = pl.ds(pl.multiple_of((tile0 + jj) * tn, tn), tn)
        return pltpu.make_async_copy(w_hbm.at[layer, :, cols], wf_ref, sem)

    @pl.when(pl.program_id(1) == 0)
    def _():
        @pl.when(j == 0)
        def _():
            copy(0).start()

        copy(j).wait()

        def body(r, carry):
            rows = pl.ds(pl.multiple_of(r * CAST_ROWS, CAST_ROWS), CAST_ROWS)
            wb_ref[rows, :] = wf_ref[rows, :].astype(BF16)
            return carry
        lax.fori_loop(0, wb_ref.shape[0] // CAST_ROWS, body, 0)

        @pl.when(j + 1 < pl.num_programs(0))
        def _():
            copy(j + 1).start()


def _weight_stream_scratch(k, tn):
    return [pltpu.VMEM((k, tn), F32), pltpu.VMEM((k, tn), BF16), pltpu.SemaphoreType.DMA(())]


def _head_norm_rope(x, g, cos, sin_lo, sin_hi):
    y = _rms(x) * g
    return (y * cos + pltpu.roll(y, HEAD_DIM - 32, axis=1) * sin_lo
            + pltpu.roll(y, 32, axis=1) * sin_hi)


INPROJ_TM, INPROJ_TN = 1024, 1024
MXU_WIDTH = 256
ROPE_ROW_CHUNK = 256
DOT_ROW_CHUNK = 256


def _inproj_heads_kernel(h_ref, w_hbm, g_ref, cos_ref, slo_ref, shi_ref, o_ref, wf_ref, wb_ref, sem, *,
                         layer, tile0, n_rope_heads, gain_scale):
    _prefetched_weight_tile(w_hbm, wf_ref, wb_ref, sem, layer=layer, tile0=tile0)
    cos, slo, shi = cos_ref[...], slo_ref[...], shi_ref[...]
    g = g_ref[...] * gain_scale
    for c0 in range(0, o_ref.shape[1], MXU_WIDTH):
        for r0 in range(0, h_ref.shape[0], ROPE_ROW_CHUNK):
            rows = slice(r0, r0 + ROPE_ROW_CHUNK)
            acc = jnp.dot(h_ref[rows, :], wb_ref[:, c0:c0 + MXU_WIDTH], preferred_element_type=F32)
            for h in range(MXU_WIDTH // HEAD_DIM):
                sl = slice(c0 + h * HEAD_DIM, c0 + (h + 1) * HEAD_DIM)
                part = acc[:, h * HEAD_DIM:(h + 1) * HEAD_DIM]
                if c0 // HEAD_DIM + h < n_rope_heads:
                    o_ref[rows, sl] = _head_norm_rope(part, g, cos[rows], slo[rows], shi[rows]).astype(BF16)
                else:
                    o_ref[rows, sl] = part.astype(BF16)


def _inproj_plain_kernel(h_ref, w_hbm, o_ref, wf_ref, wb_ref, sem, *, layer, tile0):
    _prefetched_weight_tile(w_hbm, wf_ref, wb_ref, sem, layer=layer, tile0=tile0)
    o_ref[...] = jnp.dot(h_ref[...], wb_ref[...], preferred_element_type=F32).astype(BF16)


def _inproj(h, w_all, layer, col0, width, gain=None, tables=None, n_rope_heads=0, gain_scale=1.0):
    m, d = h.shape
    tm, tn = INPROJ_TM, INPROJ_TN
    assert col0 % tn == 0 and width % tn == 0 and SEQ % tm == 0
    in_specs = [
        pl.BlockSpec((tm, d), lambda j, i: (i, 0)),
        pl.BlockSpec(memory_space=pl.ANY),
    ]
    args = [h, w_all]
    if gain is None:
        body = functools.partial(_inproj_plain_kernel, layer=layer, tile0=col0 // tn)
    else:
        body = functools.partial(_inproj_heads_kernel, layer=layer, tile0=col0 // tn,
                                 n_rope_heads=n_rope_heads, gain_scale=gain_scale)
        tab = pl.BlockSpec((tm, HEAD_DIM), lambda j, i: (i % (SEQ // tm), 0))
        in_specs += [pl.BlockSpec((1, HEAD_DIM), lambda j, i: (0, 0)), tab, tab, tab]
        args += [gain.reshape(1, HEAD_DIM), *tables]
    return pl.pallas_call(
        body,
        grid=(width // tn, m // tm),
        in_specs=in_specs,
        out_specs=pl.BlockSpec((tm, tn), lambda j, i: (i, j)),
        out_shape=jax.ShapeDtypeStruct((m, width), BF16),
        scratch_shapes=_weight_stream_scratch(d, tn),
        compiler_params=_params("arbitrary", "arbitrary"),
        name="inproj_plain" if gain is None else f"inproj_rope{n_rope_heads}",
    )(*args)


def _attn_kernel(q_ref, k_ref, v_ref, o_ref, v1_ref):
    @pl.when(pl.program_id(2) == 0)
    def _():
        v1_ref[:, :HEAD_DIM] = v_ref[...]
        v1_ref[:, HEAD_DIM:] = jnp.ones((SEQ, HEAD_DIM), BF16)

    k = k_ref[...]
    v1 = v1_ref[...]
    heads = [slice(g * HEAD_DIM, (g + 1) * HEAD_DIM) for g in range(Q_GROUP)]
    scores = [lax.dot_general(q_ref[:, sl], k, (((1,), (1,)), ((), ())), preferred_element_type=F32)
              for sl in heads]
    for sl, s in zip(heads, scores):
        p = jnp.exp2(s - jnp.max(s, axis=-1, keepdims=True))
        o = jnp.dot(p.astype(BF16), v1, preferred_element_type=F32)
        o_ref[:, sl] = (o[:, :HEAD_DIM] / o[:, HEAD_DIM:HEAD_DIM + 1]).astype(BF16)


def _attention(q, kv, batch):
    m = q.shape[0]
    tq = 512
    nq = SEQ // tq
    gw = Q_GROUP * HEAD_DIM
    return pl.pallas_call(
        _attn_kernel,
        grid=(batch, N_KV_HEADS, nq),
        in_specs=[
            pl.BlockSpec((tq, gw), lambda b, h, i: (b * nq + i, h)),
            pl.BlockSpec((SEQ, HEAD_DIM), lambda b, h, i: (b, h)),
            pl.BlockSpec((SEQ, HEAD_DIM), lambda b, h, i: (b, N_KV_HEADS + h)),
        ],
        out_specs=pl.BlockSpec((tq, gw), lambda b, h, i: (b * nq + i, h)),
        out_shape=jax.ShapeDtypeStruct((m, ATTN_WIDTH), BF16),
        scratch_shapes=[pltpu.VMEM((SEQ, 2 * HEAD_DIM), BF16)],
        compiler_params=_params("arbitrary", "arbitrary", "arbitrary"),
        name="attention",
    )(q, kv, kv)


def _conv_kernel(u_ref, gb_ref, gc_ref, up_ref, gcp_ref, un_ref, gcn_ref, w_ref, o_ref, *, tr):
    i = pl.program_id(0)
    cu = gc_ref[...].astype(F32) * u_ref[...].astype(F32)
    starts_seq = (i * tr) % SEQ == 0
    ends_seq = ((i + 1) * tr) % SEQ == 0
    prev_row = gcp_ref[7:8, :].astype(F32) * up_ref[7:8, :].astype(F32)
    next_row = gcn_ref[0:1, :].astype(F32) * un_ref[0:1, :].astype(F32)
    prev_row = jnp.where(starts_seq, 0.0, prev_row)
    next_row = jnp.where(ends_seq, 0.0, next_row)
    row = lax.broadcasted_iota(jnp.int32, cu.shape, 0)
    cu_prev = jnp.where(row == 0, prev_row, pltpu.roll(cu, 1, axis=0))
    cu_next = jnp.where(row == tr - 1, next_row, pltpu.roll(cu, tr - 1, axis=0))
    y = w_ref[0, 0:1, :] * cu_prev + w_ref[0, 1:2, :] * cu + w_ref[0, 2:3, :] * cu_next
    o_ref[...] = (gb_ref[...].astype(F32) * y).astype(BF16)


def _gated_conv(proj, conv_w_all, layer):
    m = proj.shape[0]
    tr, tc = 1024, 1024
    nrb = m // 8
    cu, cgb, cgc = REST_U // tc, REST_GB // tc, REST_GC // tc
    main = lambda off: pl.BlockSpec((tr, tc), lambda i, j: (i, off + j))
    prev = lambda off: pl.BlockSpec((8, tc), lambda i, j: (jnp.maximum(i * (tr // 8) - 1, 0), off + j))
    nxt = lambda off: pl.BlockSpec((8, tc), lambda i, j: (jnp.minimum((i + 1) * (tr // 8), nrb - 1), off + j))
    return pl.pallas_call(
        functools.partial(_conv_kernel, tr=tr),
        grid=(m // tr, CONV_WIDTH // tc),
        in_specs=[main(cu), main(cgb), main(cgc), prev(cu), prev(cgc), nxt(cu), nxt(cgc),
                  pl.BlockSpec((1, 3, tc), lambda i, j: (layer, 0, j))],
        out_specs=pl.BlockSpec((tr, tc), lambda i, j: (i, j)),
        out_shape=jax.ShapeDtypeStruct((m, CONV_WIDTH), BF16),
        compiler_params=_params("arbitrary", "arbitrary"),
        name="gated_conv",
    )(proj, proj, proj, proj, proj, proj, proj, conv_w_all)


MERGE_TM, MERGE_TN = 1024, 1024


def _merge_kernel(a_ref, c_ref, wa_hbm, wc_hbm, ga_ref, gv_ref, o_ref,
                  waf_ref, wab_ref, sema, wcf_ref, wcb_ref, semc, *, layer):
    _prefetched_weight_tile(wa_hbm, waf_ref, wab_ref, sema, layer=layer, tile0=0)
    _prefetched_weight_tile(wc_hbm, wcf_ref, wcb_ref, semc, layer=layer, tile0=0)
    for r0 in range(0, a_ref.shape[0], DOT_ROW_CHUNK):
        rows = slice(r0, r0 + DOT_ROW_CHUNK)
        ya = jnp.dot(a_ref[rows, :], wab_ref[...], preferred_element_type=F32)
        yc = jnp.dot(c_ref[rows, :], wcb_ref[...], preferred_element_type=F32)
        ga = jax.nn.sigmoid(ga_ref[rows, :].astype(F32))
        gv = jax.nn.sigmoid(gv_ref[rows, :].astype(F32))
        o_ref[rows, :] = (ga * ya + gv * yc).astype(BF16)


def _merge(attn, conv, wa_all, wc_all, layer, proj):
    m = attn.shape[0]
    tm, tn = MERGE_TM, MERGE_TN
    return pl.pallas_call(
        functools.partial(_merge_kernel, layer=layer),
        grid=(D_MODEL // tn, m // tm),
        in_specs=[
            pl.BlockSpec((tm, ATTN_WIDTH), lambda j, i: (i, 0)),
            pl.BlockSpec((tm, CONV_WIDTH), lambda j, i: (i, 0)),
            pl.BlockSpec(memory_space=pl.ANY),
            pl.BlockSpec(memory_space=pl.ANY),
            pl.BlockSpec((tm, tn), lambda j, i: (i, REST_GA // tn + j)),
            pl.BlockSpec((tm, tn), lambda j, i: (i, REST_GV // tn + j)),
        ],
        out_specs=pl.BlockSpec((tm, tn), lambda j, i: (i, j)),
        out_shape=jax.ShapeDtypeStruct((m, D_MODEL), BF16),
        scratch_shapes=_weight_stream_scratch(ATTN_WIDTH, tn) + _weight_stream_scratch(CONV_WIDTH, tn),
        compiler_params=_params("arbitrary", "arbitrary"),
        name="merge",
    )(attn, conv, wa_all, wc_all, proj, proj)


def _outproj_kernel(a_ref, w_hbm, x_ref, m_ref, o_ref, wf_ref, wb_ref, sem, *, layer, gate_row):
    _prefetched_weight_tile(w_hbm, wf_ref, wb_ref, sem, layer=layer, tile0=0)
    gate = 1.0 + m_ref[0, gate_row:gate_row + 1, :]
    for r0 in range(0, a_ref.shape[0], DOT_ROW_CHUNK):
        rows = slice(r0, r0 + DOT_ROW_CHUNK)
        y = jnp.dot(a_ref[rows, :], wb_ref[...], preferred_element_type=F32)
        o_ref[rows, :] = x_ref[rows, :] + gate * y


def _outproj(a, w_all, layer, x, mod_l, gate_row):
    m, k = a.shape
    n = w_all.shape[2]
    tm, tn = MERGE_TM, MERGE_TN
    return pl.pallas_call(
        functools.partial(_outproj_kernel, layer=layer, gate_row=gate_row),
        grid=(n // tn, m // tm),
        in_specs=[
            pl.BlockSpec((tm, k), lambda j, i: (i, 0)),
            pl.BlockSpec(memory_space=pl.ANY),
            pl.BlockSpec((tm, tn), lambda j, i: (i, j)),
            pl.BlockSpec((1, N_MOD, tn), lambda j, i: (i * tm // SEQ, 0, j)),
        ],
        out_specs=pl.BlockSpec((tm, tn), lambda j, i: (i, j)),
        out_shape=jax.ShapeDtypeStruct((m, n), F32),
        scratch_shapes=_weight_stream_scratch(k, tn),
        compiler_params=_params("arbitrary", "arbitrary"),
        name="outproj",
    )(a, w_all, x, mod_l)


def _mlp_in_kernel(h_ref, w_hbm, w2_ref, o_ref, w2b_ref, wf_ref, wb_ref, sem, *, layer):
    _prefetched_weight_tile(w_hbm, wf_ref, wb_ref, sem, layer=layer, tile0=0)
    w2b_ref[...] = w2_ref[0].astype(BF16)
    y = jnp.maximum(jnp.dot(h_ref[...], wb_ref[...], preferred_element_type=F32), 0.0)
    o_ref[...] = (y * y).astype(BF16)


def _mlp_in(h, w_all, w2_all, layer):
    m, d = h.shape
    n = w_all.shape[2]
    tm, tn = INPROJ_TM, INPROJ_TN
    nm = m // tm
    slab = w2_all.shape[1] // ((n // tn) * nm)
    assert slab * (n // tn) * nm == w2_all.shape[1] and slab % 16 == 0
    return pl.pallas_call(
        functools.partial(_mlp_in_kernel, layer=layer),
        grid=(n // tn, nm),
        in_specs=[
            pl.BlockSpec((tm, d), lambda j, i: (i, 0)),
            pl.BlockSpec(memory_space=pl.ANY),
            pl.BlockSpec((1, slab, w2_all.shape[2]), lambda j, i: (layer, j * nm + i, 0)),
        ],
        out_specs=[
            pl.BlockSpec((tm, tn), lambda j, i: (i, j)),
            pl.BlockSpec((slab, w2_all.shape[2]), lambda j, i: (j * nm + i, 0)),
        ],
        out_shape=[jax.ShapeDtypeStruct((m, n), BF16),
                   jax.ShapeDtypeStruct(w2_all.shape[1:], BF16)],
        scratch_shapes=_weight_stream_scratch(d, tn),
        compiler_params=_params("arbitrary", "arbitrary"),
        name="mlp_in",
    )(h, w_all, w2_all)


def _mlp_out_kernel(a_ref, w_ref, x_ref, m_ref, o_ref, *, gate_row):
    k = pl.program_id(2)
    last = pl.num_programs(2) - 1

    def partial_product():
        return jnp.dot(a_ref[...], w_ref[...], preferred_element_type=F32)

    @pl.when(k == 0)
    def _():
        o_ref[...] = partial_product()

    @pl.when((k > 0) & (k < last))
    def _():
        o_ref[...] += partial_product()

    @pl.when(k == last)
    def _():
        y = o_ref[...] + partial_product()
        o_ref[...] = x_ref[...] + (1.0 + m_ref[0, gate_row:gate_row + 1, :]) * y


def _mlp_out(a, w, x, mod_l, gate_row):
    m, kdim = a.shape
    n = w.shape[1]
    tm, tn, tk = 1024, 1024, 4096
    return pl.pallas_call(
        functools.partial(_mlp_out_kernel, gate_row=gate_row),
        grid=(m // tm, n // tn, kdim // tk),
        in_specs=[
            pl.BlockSpec((tm, tk), lambda i, j, k: (i, k)),
            pl.BlockSpec((tk, tn), lambda i, j, k: (k, j)),
            pl.BlockSpec((tm, tn), lambda i, j, k: (i, j)),
            pl.BlockSpec((1, N_MOD, tn), lambda i, j, k: (i * tm // SEQ, 0, j)),
        ],
        out_specs=pl.BlockSpec((tm, tn), lambda i, j, k: (i, j)),
        out_shape=jax.ShapeDtypeStruct((m, n), F32),
        compiler_params=_params("arbitrary", "arbitrary", "arbitrary"),
        name="mlp_out",
    )(a, w, x, mod_l)


def _rope_tables(seq):
    rows = seq // GRID_W
    row = jnp.repeat(jnp.arange(rows, dtype=F32), GRID_W)
    col = jnp.tile(jnp.arange(GRID_W, dtype=F32), rows)
    half = HEAD_DIM // 2
    freqs = ROPE_THETA ** (-jnp.arange(0, half, 2, dtype=F32) / half)
    ang_r = row[:, None] * freqs
    ang_c = col[:, None] * freqs
    zeros = jnp.zeros_like(ang_r)
    cos = jnp.concatenate([jnp.cos(ang_r), jnp.cos(ang_r), jnp.cos(ang_c), jnp.cos(ang_c)], axis=1)
    sin_lo = jnp.concatenate([-jnp.sin(ang_r), zeros, -jnp.sin(ang_c), zeros], axis=1)
    sin_hi = jnp.concatenate([zeros, jnp.sin(ang_r), zeros, jnp.sin(ang_c)], axis=1)
    return cos, sin_lo, sin_hi


def kernel(x, c, mod_w, mod_b, norm1_g, norm2_g, w_in, q_norm_g, k_norm_g, conv_w,
           w_branch_attn, w_branch_conv, w_out, w_mlp_in, w_mlp_out, final_norm_g):
    batch, seq, d = x.shape
    depth = mod_w.shape[0]
    assert (seq, d) == (SEQ, D_MODEL) and batch <= MOD_ROWS and w_in.shape[2] == IN_WIDTH
    m = batch * seq

    c_pad = jnp.zeros((MOD_ROWS, d), F32).at[:batch].set(c)
    mod = _modulation(c_pad, mod_w, mod_b)
    mod = mod[:, :batch].reshape(depth, batch, N_MOD, d)
    cos, sin_lo, sin_hi = _rope_tables(seq)

    xf = x.reshape(m, d)
    for l in range(depth):
        mod_l = mod[l]
        h = _norm_mod(xf, norm1_g[l], mod_l, sh_row=0, sc_row=1)
        tables = (cos, sin_lo, sin_hi)
        q = _inproj(h, w_in, l, 0, ATTN_WIDTH, q_norm_g[l], tables,
                    n_rope_heads=INPROJ_TN // HEAD_DIM, gain_scale=HEAD_DIM ** -0.5 * LOG2_E)
        kv = _inproj(h, w_in, l, COL_KV, 2 * KV_WIDTH, k_norm_g[l], tables, n_rope_heads=N_KV_HEADS)
        proj = _inproj(h, w_in, l, COL_REST, REST_WIDTH)
        attn = _attention(q, kv, batch)
        conv = _gated_conv(proj, conv_w, l)
        merged = _merge(attn, conv, w_branch_attn, w_branch_conv, l, proj)
        xf = _outproj(merged, w_out, l, xf, mod_l, gate_row=2)
        h = _norm_mod(xf, norm2_g[l], mod_l, sh_row=3, sc_row=4)
        a, w2b = _mlp_in(h, w_mlp_in, w_mlp_out, l)
        xf = _mlp_out(a, w2b, xf, mod_l, gate_row=5)
    return _final_norm(xf, final_norm_g).reshape(batch, seq, d)
```

```python
import functools
import math

import jax
import jax.numpy as jnp
from jax import lax
from jax.experimental import pallas as pl
from jax.experimental.pallas import tpu as pltpu

F32 = jnp.float32
BF16 = jnp.bfloat16

D_MODEL = 4096
SEQ = 4096
HEAD_DIM = 128
N_Q_HEADS = 16
N_KV_HEADS = 4
Q_GROUP = N_Q_HEADS // N_KV_HEADS
ATTN_WIDTH = N_Q_HEADS * HEAD_DIM
KV_WIDTH = N_KV_HEADS * HEAD_DIM
CONV_WIDTH = D_MODEL // 2
D_FF = 4 * D_MODEL
GRID_W = 64
ROPE_THETA = 10000.0
NORM_EPS = 1e-6
N_MOD = 6
LOG2_E = math.log2(math.e)

COL_KV = ATTN_WIDTH
COL_REST = COL_KV + 2 * KV_WIDTH
REST_U = 0
REST_GB = REST_U + CONV_WIDTH
REST_GC = REST_GB + CONV_WIDTH
REST_GA = REST_GC + CONV_WIDTH
REST_GV = REST_GA + D_MODEL
REST_WIDTH = REST_GV + D_MODEL
IN_WIDTH = COL_REST + REST_WIDTH

V7X_VMEM_LIMIT_BYTES = 58 * 1024 * 1024
MOD_ROWS = 16


def _params(*sem, vmem_limit_bytes=V7X_VMEM_LIMIT_BYTES):
    return pltpu.CompilerParams(dimension_semantics=sem, vmem_limit_bytes=vmem_limit_bytes)


def _mod_kernel(c_ref, w_ref, b_ref, o_ref):
    c = c_ref[...]
    a = (c * jax.nn.sigmoid(c)).astype(BF16)
    w = w_ref[0].astype(BF16)
    o_ref[0] = jnp.dot(a, w, preferred_element_type=F32) + b_ref[0]


def _modulation(c_pad, mod_w, mod_b):
    depth, d, n = mod_w.shape
    tn = 512
    return pl.pallas_call(
        _mod_kernel,
        grid=(depth, n // tn),
        in_specs=[
            pl.BlockSpec((MOD_ROWS, d), lambda l, j: (0, 0)),
            pl.BlockSpec((1, d, tn), lambda l, j: (l, 0, j)),
            pl.BlockSpec((1, 1, tn), lambda l, j: (l, 0, j)),
        ],
        out_specs=pl.BlockSpec((1, MOD_ROWS, tn), lambda l, j: (l, 0, j)),
        out_shape=jax.ShapeDtypeStruct((depth, MOD_ROWS, n), F32),
        compiler_params=_params("arbitrary", "arbitrary"),
        name="modulation",
    )(c_pad, mod_w, mod_b.reshape(depth, 1, n))


def _rms(x):
    return x * lax.rsqrt(jnp.mean(x * x, axis=-1, keepdims=True) + NORM_EPS)


NORM_ROW_CHUNK = 16


def _norm_mod_kernel(x_ref, g_ref, m_ref, o_ref, *, sh_row, sc_row):
    gain = g_ref[...]
    scale = 1.0 + m_ref[0, sc_row:sc_row + 1, :]
    shift = m_ref[0, sh_row:sh_row + 1, :]

    def body(c, carry):
        rows = pl.ds(pl.multiple_of(c * NORM_ROW_CHUNK, NORM_ROW_CHUNK), NORM_ROW_CHUNK)
        y = _rms(x_ref[rows, :]) * gain
        o_ref[rows, :] = (y * scale + shift).astype(BF16)
        return carry
    lax.fori_loop(0, x_ref.shape[0] // NORM_ROW_CHUNK, body, 0, unroll=4)


def _norm_mod(x, g, mod_l, sh_row, sc_row):
    m, d = x.shape
    tr = 1024
    return pl.pallas_call(
        functools.partial(_norm_mod_kernel, sh_row=sh_row, sc_row=sc_row),
        grid=(m // tr,),
        in_specs=[
            pl.BlockSpec((tr, d), lambda i: (i, 0)),
            pl.BlockSpec((1, d), lambda i: (0, 0)),
            pl.BlockSpec((1, N_MOD, d), lambda i: (i * tr // SEQ, 0, 0)),
        ],
        out_specs=pl.BlockSpec((tr, d), lambda i: (i, 0)),
        out_shape=jax.ShapeDtypeStruct((m, d), BF16),
        compiler_params=_params("arbitrary"),
        name="norm_mod",
    )(x, g.reshape(1, d), mod_l)


def _final_norm_kernel(x_ref, g_ref, o_ref):
    gain = g_ref[...]

    def body(c, carry):
        rows = pl.ds(pl.multiple_of(c * NORM_ROW_CHUNK, NORM_ROW_CHUNK), NORM_ROW_CHUNK)
        o_ref[rows, :] = _rms(x_ref[rows, :]) * gain
        return carry
    lax.fori_loop(0, x_ref.shape[0] // NORM_ROW_CHUNK, body, 0, unroll=4)


def _final_norm(x, g):
    m, d = x.shape
    tr = 512
    return pl.pallas_call(
        _final_norm_kernel,
        grid=(m // tr,),
        in_specs=[pl.BlockSpec((tr, d), lambda i: (i, 0)), pl.BlockSpec((1, d), lambda i: (0, 0))],
        out_specs=pl.BlockSpec((tr, d), lambda i: (i, 0)),
        out_shape=jax.ShapeDtypeStruct((m, d), F32),
        compiler_params=_params("arbitrary"),
        name="final_norm",
    )(x, g.reshape(1, d))


CAST_ROWS = 256


def _prefetched_weight_tile(w_hbm, wf_ref, wb_ref, sem, *, layer, tile0):
    j = pl.program_id(0)
    tn = wf_ref.shape[1]

    def copy(jj):
        cols = pl.ds(pl.multiple_of((tile0 + jj) * tn, tn), tn)
        return pltpu.make_async_copy(w_hbm.at[layer, :, cols], wf_ref, sem)

    @pl.when(pl.program_id(1) == 0)
    def _():
        @pl.when(j == 0)
        def _():
            copy(0).start()

        copy(j).wait()

        def body(r, carry):
            rows = pl.ds(pl.multiple_of(r * CAST_ROWS, CAST_ROWS), CAST_ROWS)
            wb_ref[rows, :] = wf_ref[rows, :].astype(BF16)
            return carry
        lax.fori_loop(0, wb_ref.shape[0] // CAST_ROWS, body, 0)

        @pl.when(j + 1 < pl.num_programs(0))
        def _():
            copy(j + 1).start()


def _weight_stream_scratch(k, tn):
    return [pltpu.VMEM((k, tn), F32), pltpu.VMEM((k, tn), BF16), pltpu.SemaphoreType.DMA(())]


def _head_norm_rope(x, g, cos, sin_lo, sin_hi):
    y = _rms(x) * g
    return (y * cos + pltpu.roll(y, HEAD_DIM - 32, axis=1) * sin_lo
            + pltpu.roll(y, 32, axis=1) * sin_hi)


INPROJ_TM, INPROJ_TN = 1024, 1024
MXU_WIDTH = 256
ROPE_ROW_CHUNK = 256
DOT_ROW_CHUNK = 256


def _inproj_heads_kernel(h_ref, w_hbm, g_ref, cos_ref, slo_ref, shi_ref, o_ref, wf_ref, wb_ref, sem, *,
                         layer, tile0, n_rope_heads, gain_scale):
    _prefetched_weight_tile(w_hbm, wf_ref, wb_ref, sem, layer=layer, tile0=tile0)
    cos, slo, shi = cos_ref[...], slo_ref[...], shi_ref[...]
    g = g_ref[...] * gain_scale
    for c0 in range(0, o_ref.shape[1], MXU_WIDTH):
        for r0 in range(0, h_ref.shape[0], ROPE_ROW_CHUNK):
            rows = slice(r0, r0 + ROPE_ROW_CHUNK)
            acc = jnp.dot(h_ref[rows, :], wb_ref[:, c0:c0 + MXU_WIDTH], preferred_element_type=F32)
            for h in range(MXU_WIDTH // HEAD_DIM):
                sl = slice(c0 + h * HEAD_DIM, c0 + (h + 1) * HEAD_DIM)
                part = acc[:, h * HEAD_DIM:(h + 1) * HEAD_DIM]
                if c0 // HEAD_DIM + h < n_rope_heads:
                    o_ref[rows, sl] = _head_norm_rope(part, g, cos[rows], slo[rows], shi[rows]).astype(BF16)
                else:
                    o_ref[rows, sl] = part.astype(BF16)


def _inproj_plain_kernel(h_ref, w_hbm, o_ref, wf_ref, wb_ref, sem, *, layer, tile0):
    _prefetched_weight_tile(w_hbm, wf_ref, wb_ref, sem, layer=layer, tile0=tile0)
    o_ref[...] = jnp.dot(h_ref[...], wb_ref[...], preferred_element_type=F32).astype(BF16)


def _inproj(h, w_all, layer, col0, width, gain=None, tables=None, n_rope_heads=0, gain_scale=1.0):
    m, d = h.shape
    tm, tn = INPROJ_TM, INPROJ_TN
    assert col0 % tn == 0 and width % tn == 0 and SEQ % tm == 0
    in_specs = [
        pl.BlockSpec((tm, d), lambda j, i: (i, 0)),
        pl.BlockSpec(memory_space=pl.ANY),
    ]
    args = [h, w_all]
    if gain is None:
        body = functools.partial(_inproj_plain_kernel, layer=layer, tile0=col0 // tn)
    else:
        body = functools.partial(_inproj_heads_kernel, layer=layer, tile0=col0 // tn,
                                 n_rope_heads=n_rope_heads, gain_scale=gain_scale)
        tab = pl.BlockSpec((tm, HEAD_DIM), lambda j, i: (i % (SEQ // tm), 0))
        in_specs += [pl.BlockSpec((1, HEAD_DIM), lambda j, i: (0, 0)), tab, tab, tab]
        args += [gain.reshape(1, HEAD_DIM), *tables]
    if gain is None:
        out_specs = pl.BlockSpec((None, tm, tn), lambda j, i: (j, i, 0))
        out_shape = jax.ShapeDtypeStruct((width // tn, m, tn), BF16)
    else:
        out_specs = pl.BlockSpec((tm, tn), lambda j, i: (i, j))
        out_shape = jax.ShapeDtypeStruct((m, width), BF16)
    return pl.pallas_call(
        body,
        grid=(width // tn, m // tm),
        in_specs=in_specs,
        out_specs=out_specs,
        out_shape=out_shape,
        scratch_shapes=_weight_stream_scratch(d, tn),
        compiler_params=_params("arbitrary", "arbitrary"),
        name="inproj_plain" if gain is None else f"inproj_rope{n_rope_heads}",
    )(*args)


def _attn_kernel(q_ref, k_ref, v_ref, o_ref, v1_ref):
    @pl.when(pl.program_id(2) == 0)
    def _():
        v1_ref[:, :HEAD_DIM] = v_ref[...]
        v1_ref[:, HEAD_DIM:] = jnp.ones((SEQ, HEAD_DIM), BF16)

    k = k_ref[...]
    v1 = v1_ref[...]
    heads = [slice(g * HEAD_DIM, (g + 1) * HEAD_DIM) for g in range(Q_GROUP)]
    scores = [lax.dot_general(q_ref[:, sl], k, (((1,), (1,)), ((), ())), preferred_element_type=F32)
              for sl in heads]
    for sl, s in zip(heads, scores):
        p = jnp.exp2(s - jnp.max(s, axis=-1, keepdims=True))
        o = jnp.dot(p.astype(BF16), v1, preferred_element_type=F32)
        o_ref[:, sl] = (o[:, :HEAD_DIM] / o[:, HEAD_DIM:HEAD_DIM + 1]).astype(BF16)


def _attention(q, kv, batch):
    m = q.shape[0]
    tq = 512
    nq = SEQ // tq
    gw = Q_GROUP * HEAD_DIM
    return pl.pallas_call(
        _attn_kernel,
        grid=(batch, N_KV_HEADS, nq),
        in_specs=[
            pl.BlockSpec((tq, gw), lambda b, h, i: (b * nq + i, h)),
            pl.BlockSpec((SEQ, HEAD_DIM), lambda b, h, i: (b, h)),
            pl.BlockSpec((SEQ, HEAD_DIM), lambda b, h, i: (b, N_KV_HEADS + h)),
        ],
        out_specs=pl.BlockSpec((tq, gw), lambda b, h, i: (b * nq + i, h)),
        out_shape=jax.ShapeDtypeStruct((m, ATTN_WIDTH), BF16),
        scratch_shapes=[pltpu.VMEM((SEQ, 2 * HEAD_DIM), BF16)],
        compiler_params=_params("arbitrary", "arbitrary", "arbitrary"),
        name="attention",
    )(q, kv, kv)


def _conv_kernel(u_ref, gb_ref, gc_ref, up_ref, gcp_ref, un_ref, gcn_ref, w_ref, o_ref, *, tr):
    i = pl.program_id(0)
    cu = gc_ref[...].astype(F32) * u_ref[...].astype(F32)
    starts_seq = (i * tr) % SEQ == 0
    ends_seq = ((i + 1) * tr) % SEQ == 0
    prev_row = gcp_ref[7:8, :].astype(F32) * up_ref[7:8, :].astype(F32)
    next_row = gcn_ref[0:1, :].astype(F32) * un_ref[0:1, :].astype(F32)
    prev_row = jnp.where(starts_seq, 0.0, prev_row)
    next_row = jnp.where(ends_seq, 0.0, next_row)
    row = lax.broadcasted_iota(jnp.int32, cu.shape, 0)
    cu_prev = jnp.where(row == 0, prev_row, pltpu.roll(cu, 1, axis=0))
    cu_next = jnp.where(row == tr - 1, next_row, pltpu.roll(cu, tr - 1, axis=0))
    y = w_ref[0, 0:1, :] * cu_prev + w_ref[0, 1:2, :] * cu + w_ref[0, 2:3, :] * cu_next
    o_ref[...] = (gb_ref[...].astype(F32) * y).astype(BF16)


def _gated_conv(proj, conv_w_all, layer):
    m, tc = proj.shape[1:]
    tr = 1024
    nrb = m // 8
    cu, cgb, cgc = REST_U // tc, REST_GB // tc, REST_GC // tc
    main = lambda off: pl.BlockSpec((None, tr, tc), lambda i, j: (off + j, i, 0))
    prev = lambda off: pl.BlockSpec((None, 8, tc), lambda i, j: (off + j, jnp.maximum(i * (tr // 8) - 1, 0), 0))
    nxt = lambda off: pl.BlockSpec((None, 8, tc),
                                   lambda i, j: (off + j, jnp.minimum((i + 1) * (tr // 8), nrb - 1), 0))
    return pl.pallas_call(
        functools.partial(_conv_kernel, tr=tr),
        grid=(m // tr, CONV_WIDTH // tc),
        in_specs=[main(cu), main(cgb), main(cgc), prev(cu), prev(cgc), nxt(cu), nxt(cgc),
                  pl.BlockSpec((1, 3, tc), lambda i, j: (layer, 0, j))],
        out_specs=pl.BlockSpec((tr, tc), lambda i, j: (i, j)),
        out_shape=jax.ShapeDtypeStruct((m, CONV_WIDTH), BF16),
        compiler_params=_params("arbitrary", "arbitrary"),
        name="gated_conv",
    )(proj, proj, proj, proj, proj, proj, proj, conv_w_all)


MERGE_TM, MERGE_TN = 1024, 1024


def _merge_kernel(a_ref, c_ref, wa_hbm, wc_hbm, ga_ref, gv_ref, o_ref,
                  waf_ref, wab_ref, sema, wcf_ref, wcb_ref, semc, *, layer):
    _prefetched_weight_tile(wa_hbm, waf_ref, wab_ref, sema, layer=layer, tile0=0)
    _prefetched_weight_tile(wc_hbm, wcf_ref, wcb_ref, semc, layer=layer, tile0=0)
    for r0 in range(0, a_ref.shape[0], DOT_ROW_CHUNK):
        rows = slice(r0, r0 + DOT_ROW_CHUNK)
        ya = jnp.dot(a_ref[rows, :], wab_ref[...], preferred_element_type=F32)
        yc = jnp.dot(c_ref[rows, :], wcb_ref[...], preferred_element_type=F32)
        ga = jax.nn.sigmoid(ga_ref[rows, :].astype(F32))
        gv = jax.nn.sigmoid(gv_ref[rows, :].astype(F32))
        o_ref[rows, :] = (ga * ya + gv * yc).astype(BF16)


def _merge(attn, conv, wa_all, wc_all, layer, proj):
    m = attn.shape[0]
    tm, tn = MERGE_TM, MERGE_TN
    assert proj.shape[2] == tn
    return pl.pallas_call(
        functools.partial(_merge_kernel, layer=layer),
        grid=(D_MODEL // tn, m // tm),
        in_specs=[
            pl.BlockSpec((tm, ATTN_WIDTH), lambda j, i: (i, 0)),
            pl.BlockSpec((tm, CONV_WIDTH), lambda j, i: (i, 0)),
            pl.BlockSpec(memory_space=pl.ANY),
            pl.BlockSpec(memory_space=pl.ANY),
            pl.BlockSpec((None, tm, tn), lambda j, i: (REST_GA // tn + j, i, 0)),
            pl.BlockSpec((None, tm, tn), lambda j, i: (REST_GV // tn + j, i, 0)),
        ],
        out_specs=pl.BlockSpec((tm, tn), lambda j, i: (i, j)),
        out_shape=jax.ShapeDtypeStruct((m, D_MODEL), BF16),
        scratch_shapes=_weight_stream_scratch(ATTN_WIDTH, tn) + _weight_stream_scratch(CONV_WIDTH, tn),
        compiler_params=_params("arbitrary", "arbitrary"),
        name="merge",
    )(attn, conv, wa_all, wc_all, proj, proj)


def _outproj_kernel(a_ref, w_hbm, x_ref, m_ref, o_ref, wf_ref, wb_ref, sem, *, layer, gate_row):
    _prefetched_weight_tile(w_hbm, wf_ref, wb_ref, sem, layer=layer, tile0=0)
    gate = 1.0 + m_ref[0, gate_row:gate_row + 1, :]
    for r0 in range(0, a_ref.shape[0], DOT_ROW_CHUNK):
        rows = slice(r0, r0 + DOT_ROW_CHUNK)
        y = jnp.dot(a_ref[rows, :], wb_ref[...], preferred_element_type=F32)
        o_ref[rows, :] = x_ref[rows, :] + gate * y


def _outproj(a, w_all, layer, x, mod_l, gate_row):
    m, k = a.shape
    n = w_all.shape[2]
    tm, tn = MERGE_TM, MERGE_TN
    return pl.pallas_call(
        functools.partial(_outproj_kernel, layer=layer, gate_row=gate_row),
        grid=(n // tn, m // tm),
        in_specs=[
            pl.BlockSpec((tm, k), lambda j, i: (i, 0)),
            pl.BlockSpec(memory_space=pl.ANY),
            pl.BlockSpec((tm, tn), lambda j, i: (i, j)),
            pl.BlockSpec((1, N_MOD, tn), lambda j, i: (i * tm // SEQ, 0, j)),
        ],
        out_specs=pl.BlockSpec((tm, tn), lambda j, i: (i, j)),
        out_shape=jax.ShapeDtypeStruct((m, n), F32),
        scratch_shapes=_weight_stream_scratch(k, tn),
        compiler_params=_params("arbitrary", "arbitrary"),
        name="outproj",
    )(a, w_all, x, mod_l)


def _mlp_in_kernel(h_ref, w_hbm, w2_ref, o_ref, w2b_ref, wf_ref, wb_ref, sem, *, layer):
    _prefetched_weight_tile(w_hbm, wf_ref, wb_ref, sem, layer=layer, tile0=0)
    w2b_ref[...] = w2_ref[0].astype(BF16)
    y = jnp.maximum(jnp.dot(h_ref[...], wb_ref[...], preferred_element_type=F32), 0.0)
    o_ref[...] = (y * y).astype(BF16)


def _mlp_in(h, w_all, w2_all, layer):
    m, d = h.shape
    n = w_all.shape[2]
    tm, tn = INPROJ_TM, INPROJ_TN
    nm = m // tm
    slab = w2_all.shape[1] // ((n // tn) * nm)
    assert slab * (n // tn) * nm == w2_all.shape[1] and slab % 16 == 0
    return pl.pallas_call(
        functools.partial(_mlp_in_kernel, layer=layer),
        grid=(n // tn, nm),
        in_specs=[
            pl.BlockSpec((tm, d), lambda j, i: (i, 0)),
            pl.BlockSpec(memory_space=pl.ANY),
            pl.BlockSpec((1, slab, w2_all.shape[2]), lambda j, i: (layer, j * nm + i, 0)),
        ],
        out_specs=[
            pl.BlockSpec((tm, tn), lambda j, i: (i, j)),
            pl.BlockSpec((slab, w2_all.shape[2]), lambda j, i: (j * nm + i, 0)),
        ],
        out_shape=[jax.ShapeDtypeStruct((m, n), BF16),
                   jax.ShapeDtypeStruct(w2_all.shape[1:], BF16)],
        scratch_shapes=_weight_stream_scratch(d, tn),
        compiler_params=_params("arbitrary", "arbitrary"),
        name="mlp_in",
    )(h, w_all, w2_all)


def _mlp_out_kernel(a_ref, w_ref, x_ref, m_ref, o_ref, *, gate_row):
    k = pl.program_id(2)
    last = pl.num_programs(2) - 1

    def partial_product():
        return jnp.dot(a_ref[...], w_ref[...], preferred_element_type=F32)

    @pl.when(k == 0)
    def _():
        o_ref[...] = partial_product()

    @pl.when((k > 0) & (k < last))
    def _():
        o_ref[...] += partial_product()

    @pl.when(k == last)
    def _():
        y = o_ref[...] + partial_product()
        o_ref[...] = x_ref[...] + (1.0 + m_ref[0, gate_row:gate_row + 1, :]) * y


def _mlp_out(a, w, x, mod_l, gate_row):
    m, kdim = a.shape
    n = w.shape[1]
    tm, tn, tk = 1024, 1024, 4096
    return pl.pallas_call(
        functools.partial(_mlp_out_kernel, gate_row=gate_row),
        grid=(m // tm, n // tn, kdim // tk),
        in_specs=[
            pl.BlockSpec((tm, tk), lambda i, j, k: (i, k)),
            pl.BlockSpec((tk, tn), lambda i, j, k: (k, j)),
            pl.BlockSpec((tm, tn), lambda i, j, k: (i, j)),
            pl.BlockSpec((1, N_MOD, tn), lambda i, j, k: (i * tm // SEQ, 0, j)),
        ],
        out_specs=pl.BlockSpec((tm, tn), lambda i, j, k: (i, j)),
        out_shape=jax.ShapeDtypeStruct((m, n), F32),
        compiler_params=_params("arbitrary", "arbitrary", "arbitrary"),
        name="mlp_out",
    )(a, w, x, mod_l)


def _rope_tables(seq):
    rows = seq // GRID_W
    row = jnp.repeat(jnp.arange(rows, dtype=F32), GRID_W)
    col = jnp.tile(jnp.arange(GRID_W, dtype=F32), rows)
    half = HEAD_DIM // 2
    freqs = ROPE_THETA ** (-jnp.arange(0, half, 2, dtype=F32) / half)
    ang_r = row[:, None] * freqs
    ang_c = col[:, None] * freqs
    zeros = jnp.zeros_like(ang_r)
    cos = jnp.concatenate([jnp.cos(ang_r), jnp.cos(ang_r), jnp.cos(ang_c), jnp.cos(ang_c)], axis=1)
    sin_lo = jnp.concatenate([-jnp.sin(ang_r), zeros, -jnp.sin(ang_c), zeros], axis=1)
    sin_hi = jnp.concatenate([zeros, jnp.sin(ang_r), zeros, jnp.sin(ang_c)], axis=1)
    return cos, sin_lo, sin_hi


def kernel(x, c, mod_w, mod_b, norm1_g, norm2_g, w_in, q_norm_g, k_norm_g, conv_w,
           w_branch_attn, w_branch_conv, w_out, w_mlp_in, w_mlp_out, final_norm_g):
    batch, seq, d = x.shape
    depth = mod_w.shape[0]
    assert (seq, d) == (SEQ, D_MODEL) and batch <= MOD_ROWS and w_in.shape[2] == IN_WIDTH
    m = batch * seq

    c_pad = jnp.zeros((MOD_ROWS, d), F32).at[:batch].set(c)
    mod = _modulation(c_pad, mod_w, mod_b)
    mod = mod[:, :batch].reshape(depth, batch, N_MOD, d)
    cos, sin_lo, sin_hi = _rope_tables(seq)

    xf = x.reshape(m, d)
    for l in range(depth):
        mod_l = mod[l]
        h = _norm_mod(xf, norm1_g[l], mod_l, sh_row=0, sc_row=1)
        tables = (cos, sin_lo, sin_hi)
        q = _inproj(h, w_in, l, 0, ATTN_WIDTH, q_norm_g[l], tables,
                    n_rope_heads=INPROJ_TN // HEAD_DIM, gain_scale=HEAD_DIM ** -0.5 * LOG2_E)
        kv = _inproj(h, w_in, l, COL_KV, 2 * KV_WIDTH, k_norm_g[l], tables, n_rope_heads=N_KV_HEADS)
        proj = _inproj(h, w_in, l, COL_REST, REST_WIDTH)
        attn = _attention(q, kv, batch)
        conv = _gated_conv(proj, conv_w, l)
        merged = _merge(attn, conv, w_branch_attn, w_branch_conv, l, proj)
        xf = _outproj(merged, w_out, l, xf, mod_l, gate_row=2)
        h = _norm_mod(xf, norm2_g[l], mod_l, sh_row=3, sc_row=4)
        a, w2b = _mlp_in(h, w_mlp_in, w_mlp_out, l)
        xf = _mlp_out(a, w2b, xf, mod_l, gate_row=5)
    return _final_norm(xf, final_norm_g).reshape(batch, seq, d)
```

```python
import functools
import math

import jax
import jax.numpy as jnp
from jax import lax
from jax.experimental import pallas as pl
from jax.experimental.pallas import tpu as pltpu

F32 = jnp.float32
BF16 = jnp.bfloat16

D_MODEL = 4096
SEQ = 4096
HEAD_DIM = 128
N_Q_HEADS = 16
N_KV_HEADS = 4
Q_GROUP = N_Q_HEADS // N_KV_HEADS
ATTN_WIDTH = N_Q_HEADS * HEAD_DIM
KV_WIDTH = N_KV_HEADS * HEAD_DIM
CONV_WIDTH = D_MODEL // 2
D_FF = 4 * D_MODEL
GRID_W = 64
ROPE_THETA = 10000.0
NORM_EPS = 1e-6
N_MOD = 6
LOG2_E = math.log2(math.e)

COL_KV = ATTN_WIDTH
COL_REST = COL_KV + 2 * KV_WIDTH
REST_U = 0
REST_GB = REST_U + CONV_WIDTH
REST_GC = REST_GB + CONV_WIDTH
REST_GA = REST_GC + CONV_WIDTH
REST_GV = REST_GA + D_MODEL
REST_WIDTH = REST_GV + D_MODEL
IN_WIDTH = COL_REST + REST_WIDTH

V7X_VMEM_LIMIT_BYTES = 58 * 1024 * 1024
MOD_ROWS = 16


def _params(*sem, vmem_limit_bytes=V7X_VMEM_LIMIT_BYTES):
    return pltpu.CompilerParams(dimension_semantics=sem, vmem_limit_bytes=vmem_limit_bytes)


def _mod_kernel(c_ref, w_ref, b_ref, o_ref):
    c = c_ref[...]
    a = (c * jax.nn.sigmoid(c)).astype(BF16)
    w = w_ref[0].astype(BF16)
    o_ref[0] = jnp.dot(a, w, preferred_element_type=F32) + b_ref[0]


def _modulation(c_pad, mod_w, mod_b):
    depth, d, n = mod_w.shape
    tn = 512
    return pl.pallas_call(
        _mod_kernel,
        grid=(depth, n // tn),
        in_specs=[
            pl.BlockSpec((MOD_ROWS, d), lambda l, j: (0, 0)),
            pl.BlockSpec((1, d, tn), lambda l, j: (l, 0, j)),
            pl.BlockSpec((1, 1, tn), lambda l, j: (l, 0, j)),
        ],
        out_specs=pl.BlockSpec((1, MOD_ROWS, tn), lambda l, j: (l, 0, j)),
        out_shape=jax.ShapeDtypeStruct((depth, MOD_ROWS, n), F32),
        compiler_params=_params("arbitrary", "arbitrary"),
        name="modulation",
    )(c_pad, mod_w, mod_b.reshape(depth, 1, n))


def _rms(x):
    return x * lax.rsqrt(jnp.mean(x * x, axis=-1, keepdims=True) + NORM_EPS)


NORM_ROW_CHUNK = 16


def _norm_mod_kernel(x_ref, g_ref, m_ref, o_ref, *, sh_row, sc_row):
    gain = g_ref[...]
    scale = 1.0 + m_ref[0, sc_row:sc_row + 1, :]
    shift = m_ref[0, sh_row:sh_row + 1, :]

    def body(c, carry):
        rows = pl.ds(pl.multiple_of(c * NORM_ROW_CHUNK, NORM_ROW_CHUNK), NORM_ROW_CHUNK)
        y = _rms(x_ref[rows, :]) * gain
        o_ref[rows, :] = (y * scale + shift).astype(BF16)
        return carry
    lax.fori_loop(0, x_ref.shape[0] // NORM_ROW_CHUNK, body, 0, unroll=4)


def _norm_mod(x, g, mod_l, sh_row, sc_row):
    m, d = x.shape
    tr = 1024
    return pl.pallas_call(
        functools.partial(_norm_mod_kernel, sh_row=sh_row, sc_row=sc_row),
        grid=(m // tr,),
        in_specs=[
            pl.BlockSpec((tr, d), lambda i: (i, 0)),
            pl.BlockSpec((1, d), lambda i: (0, 0)),
            pl.BlockSpec((1, N_MOD, d), lambda i: (i * tr // SEQ, 0, 0)),
        ],
        out_specs=pl.BlockSpec((tr, d), lambda i: (i, 0)),
        out_shape=jax.ShapeDtypeStruct((m, d), BF16),
        compiler_params=_params("arbitrary"),
        name="norm_mod",
    )(x, g.reshape(1, d), mod_l)


def _final_norm_kernel(x_ref, g_ref, o_ref):
    gain = g_ref[...]

    def body(c, carry):
        rows = pl.ds(pl.multiple_of(c * NORM_ROW_CHUNK, NORM_ROW_CHUNK), NORM_ROW_CHUNK)
        o_ref[rows, :] = _rms(x_ref[rows, :]) * gain
        return carry
    lax.fori_loop(0, x_ref.shape[0] // NORM_ROW_CHUNK, body, 0, unroll=4)


def _final_norm(x, g):
    m, d = x.shape
    tr = 512
    return pl.pallas_call(
        _final_norm_kernel,
        grid=(m // tr,),
        in_specs=[pl.BlockSpec((tr, d), lambda i: (i, 0)), pl.BlockSpec((1, d), lambda i: (0, 0))],
        out_specs=pl.BlockSpec((tr, d), lambda i: (i, 0)),
        out_shape=jax.ShapeDtypeStruct((m, d), F32),
        compiler_params=_params("arbitrary"),
        name="final_norm",
    )(x, g.reshape(1, d))


CAST_ROWS = 256


def _prefetched_weight_tile(w_hbm, wf_ref, wb_ref, sem, *, layer, tile0):
    j = pl.program_id(0)
    tn = wf_ref.shape[1]

    def copy(jj):
        cols = pl.ds(pl.multiple_of((tile0 + jj) * tn, tn), tn)
        return pltpu.make_async_copy(w_hbm.at[layer, :, cols], wf_ref, sem)

    @pl.when(pl.program_id(1) == 0)
    def _():
        @pl.when(j == 0)
        def _():
            copy(0).start()

        copy(j).wait()

        def body(r, carry):
            rows = pl.ds(pl.multiple_of(r * CAST_ROWS, CAST_ROWS), CAST_ROWS)
            wb_ref[rows, :] = wf_ref[rows, :].astype(BF16)
            return carry
        lax.fori_loop(0, wb_ref.shape[0] // CAST_ROWS, body, 0)

        @pl.when(j + 1 < pl.num_programs(0))
        def _():
            copy(j + 1).start(priority=1)


def _weight_stream_scratch(k, tn):
    return [pltpu.VMEM((k, tn), F32), pltpu.VMEM((k, tn), BF16), pltpu.SemaphoreType.DMA(())]


def _head_norm_rope(x, g, cos, sin_lo, sin_hi):
    y = _rms(x) * g
    return (y * cos + pltpu.roll(y, HEAD_DIM - 32, axis=1) * sin_lo
            + pltpu.roll(y, 32, axis=1) * sin_hi)


INPROJ_TM, INPROJ_TN = 1024, 1024
MXU_WIDTH = 256
ROPE_ROW_CHUNK = 256
DOT_ROW_CHUNK = 256


def _inproj_heads_kernel(h_ref, w_hbm, g_ref, cos_ref, slo_ref, shi_ref, o_ref, wf_ref, wb_ref, sem, *,
                         layer, tile0, n_rope_heads, gain_scale):
    _prefetched_weight_tile(w_hbm, wf_ref, wb_ref, sem, layer=layer, tile0=tile0)
    cos, slo, shi = cos_ref[...], slo_ref[...], shi_ref[...]
    g = g_ref[...] * gain_scale
    for c0 in range(0, o_ref.shape[1], MXU_WIDTH):
        for r0 in range(0, h_ref.shape[0], ROPE_ROW_CHUNK):
            rows = slice(r0, r0 + ROPE_ROW_CHUNK)
            acc = jnp.dot(h_ref[rows, :], wb_ref[:, c0:c0 + MXU_WIDTH], preferred_element_type=F32)
            for h in range(MXU_WIDTH // HEAD_DIM):
                sl = slice(c0 + h * HEAD_DIM, c0 + (h + 1) * HEAD_DIM)
                part = acc[:, h * HEAD_DIM:(h + 1) * HEAD_DIM]
                if c0 // HEAD_DIM + h < n_rope_heads:
                    o_ref[rows, sl] = _head_norm_rope(part, g, cos[rows], slo[rows], shi[rows]).astype(BF16)
                else:
                    o_ref[rows, sl] = part.astype(BF16)


def _inproj_plain_kernel(h_ref, w_hbm, o_ref, wf_ref, wb_ref, sem, *, layer, tile0):
    _prefetched_weight_tile(w_hbm, wf_ref, wb_ref, sem, layer=layer, tile0=tile0)
    o_ref[...] = jnp.dot(h_ref[...], wb_ref[...], preferred_element_type=F32).astype(BF16)


def _inproj(h, w_all, layer, col0, width, gain=None, tables=None, n_rope_heads=0, gain_scale=1.0):
    m, d = h.shape
    tm, tn = INPROJ_TM, INPROJ_TN
    assert col0 % tn == 0 and width % tn == 0 and SEQ % tm == 0
    in_specs = [
        pl.BlockSpec((tm, d), lambda j, i: (i, 0)),
        pl.BlockSpec(memory_space=pl.ANY),
    ]
    args = [h, w_all]
    if gain is None:
        body = functools.partial(_inproj_plain_kernel, layer=layer, tile0=col0 // tn)
    else:
        body = functools.partial(_inproj_heads_kernel, layer=layer, tile0=col0 // tn,
                                 n_rope_heads=n_rope_heads, gain_scale=gain_scale)
        tab = pl.BlockSpec((tm, HEAD_DIM), lambda j, i: (i % (SEQ // tm), 0))
        in_specs += [pl.BlockSpec((1, HEAD_DIM), lambda j, i: (0, 0)), tab, tab, tab]
        args += [gain.reshape(1, HEAD_DIM), *tables]
    return pl.pallas_call(
        body,
        grid=(width // tn, m // tm),
        in_specs=in_specs,
        out_specs=pl.BlockSpec((tm, tn), lambda j, i: (i, j)),
        out_shape=jax.ShapeDtypeStruct((m, width), BF16),
        scratch_shapes=_weight_stream_scratch(d, tn),
        compiler_params=_params("arbitrary", "arbitrary"),
        name="inproj_plain" if gain is None else f"inproj_rope{n_rope_heads}",
    )(*args)


def _attn_kernel(q_ref, k_ref, v_ref, o_ref, v1_ref):
    @pl.when(pl.program_id(2) == 0)
    def _():
        v1_ref[:, :HEAD_DIM] = v_ref[...]
        v1_ref[:, HEAD_DIM:] = jnp.ones((SEQ, HEAD_DIM), BF16)

    k = k_ref[...]
    v1 = v1_ref[...]
    heads = [slice(g * HEAD_DIM, (g + 1) * HEAD_DIM) for g in range(Q_GROUP)]
    scores = [lax.dot_general(q_ref[:, sl], k, (((1,), (1,)), ((), ())), preferred_element_type=F32)
              for sl in heads]
    for sl, s in zip(heads, scores):
        p = jnp.exp2(s - jnp.max(s, axis=-1, keepdims=True))
        o = jnp.dot(p.astype(BF16), v1, preferred_element_type=F32)
        o_ref[:, sl] = (o[:, :HEAD_DIM] / o[:, HEAD_DIM:HEAD_DIM + 1]).astype(BF16)


def _attention(q, kv, batch):
    m = q.shape[0]
    tq = 512
    nq = SEQ // tq
    gw = Q_GROUP * HEAD_DIM
    return pl.pallas_call(
        _attn_kernel,
        grid=(batch, N_KV_HEADS, nq),
        in_specs=[
            pl.BlockSpec((tq, gw), lambda b, h, i: (b * nq + i, h)),
            pl.BlockSpec((SEQ, HEAD_DIM), lambda b, h, i: (b, h)),
            pl.BlockSpec((SEQ, HEAD_DIM), lambda b, h, i: (b, N_KV_HEADS + h)),
        ],
        out_specs=pl.BlockSpec((tq, gw), lambda b, h, i: (b * nq + i, h)),
        out_shape=jax.ShapeDtypeStruct((m, ATTN_WIDTH), BF16),
        scratch_shapes=[pltpu.VMEM((SEQ, 2 * HEAD_DIM), BF16)],
        compiler_params=_params("arbitrary", "arbitrary", "arbitrary"),
        name="attention",
    )(q, kv, kv)


def _conv_kernel(u_ref, gb_ref, gc_ref, up_ref, gcp_ref, un_ref, gcn_ref, w_ref, o_ref, *, tr):
    i = pl.program_id(0)
    cu = gc_ref[...].astype(F32) * u_ref[...].astype(F32)
    starts_seq = (i * tr) % SEQ == 0
    ends_seq = ((i + 1) * tr) % SEQ == 0
    prev_row = gcp_ref[7:8, :].astype(F32) * up_ref[7:8, :].astype(F32)
    next_row = gcn_ref[0:1, :].astype(F32) * un_ref[0:1, :].astype(F32)
    prev_row = jnp.where(starts_seq, 0.0, prev_row)
    next_row = jnp.where(ends_seq, 0.0, next_row)
    row = lax.broadcasted_iota(jnp.int32, cu.shape, 0)
    cu_prev = jnp.where(row == 0, prev_row, pltpu.roll(cu, 1, axis=0))
    cu_next = jnp.where(row == tr - 1, next_row, pltpu.roll(cu, tr - 1, axis=0))
    y = w_ref[0, 0:1, :] * cu_prev + w_ref[0, 1:2, :] * cu + w_ref[0, 2:3, :] * cu_next
    o_ref[...] = (gb_ref[...].astype(F32) * y).astype(BF16)


def _gated_conv(proj, conv_w_all, layer):
    m = proj.shape[0]
    tr, tc = 1024, 1024
    nrb = m // 8
    cu, cgb, cgc = REST_U // tc, REST_GB // tc, REST_GC // tc
    main = lambda off: pl.BlockSpec((tr, tc), lambda i, j: (i, off + j))
    prev = lambda off: pl.BlockSpec((8, tc), lambda i, j: (jnp.maximum(i * (tr // 8) - 1, 0), off + j))
    nxt = lambda off: pl.BlockSpec((8, tc), lambda i, j: (jnp.minimum((i + 1) * (tr // 8), nrb - 1), off + j))
    return pl.pallas_call(
        functools.partial(_conv_kernel, tr=tr),
        grid=(m // tr, CONV_WIDTH // tc),
        in_specs=[main(cu), main(cgb), main(cgc), prev(cu), prev(cgc), nxt(cu), nxt(cgc),
                  pl.BlockSpec((1, 3, tc), lambda i, j: (layer, 0, j))],
        out_specs=pl.BlockSpec((tr, tc), lambda i, j: (i, j)),
        out_shape=jax.ShapeDtypeStruct((m, CONV_WIDTH), BF16),
        compiler_params=_params("arbitrary", "arbitrary"),
        name="gated_conv",
    )(proj, proj, proj, proj, proj, proj, proj, conv_w_all)


MERGE_TM, MERGE_TN = 1024, 1024


def _merge_kernel(a_ref, c_ref, wa_hbm, wc_hbm, ga_ref, gv_ref, o_ref,
                  waf_ref, wab_ref, sema, wcf_ref, wcb_ref, semc, *, layer):
    _prefetched_weight_tile(wa_hbm, waf_ref, wab_ref, sema, layer=layer, tile0=0)
    _prefetched_weight_tile(wc_hbm, wcf_ref, wcb_ref, semc, layer=layer, tile0=0)
    for r0 in range(0, a_ref.shape[0], DOT_ROW_CHUNK):
        rows = slice(r0, r0 + DOT_ROW_CHUNK)
        ya = jnp.dot(a_ref[rows, :], wab_ref[...], preferred_element_type=F32)
        yc = jnp.dot(c_ref[rows, :], wcb_ref[...], preferred_element_type=F32)
        ga = jax.nn.sigmoid(ga_ref[rows, :].astype(F32))
        gv = jax.nn.sigmoid(gv_ref[rows, :].astype(F32))
        o_ref[rows, :] = (ga * ya + gv * yc).astype(BF16)


def _merge(attn, conv, wa_all, wc_all, layer, proj):
    m = attn.shape[0]
    tm, tn = MERGE_TM, MERGE_TN
    return pl.pallas_call(
        functools.partial(_merge_kernel, layer=layer),
        grid=(D_MODEL // tn, m // tm),
        in_specs=[
            pl.BlockSpec((tm, ATTN_WIDTH), lambda j, i: (i, 0)),
            pl.BlockSpec((tm, CONV_WIDTH), lambda j, i: (i, 0)),
            pl.BlockSpec(memory_space=pl.ANY),
            pl.BlockSpec(memory_space=pl.ANY),
            pl.BlockSpec((tm, tn), lambda j, i: (i, REST_GA // tn + j)),
            pl.BlockSpec((tm, tn), lambda j, i: (i, REST_GV // tn + j)),
        ],
        out_specs=pl.BlockSpec((tm, tn), lambda j, i: (i, j)),
        out_shape=jax.ShapeDtypeStruct((m, D_MODEL), BF16),
        scratch_shapes=_weight_stream_scratch(ATTN_WIDTH, tn) + _weight_stream_scratch(CONV_WIDTH, tn),
        compiler_params=_params("arbitrary", "arbitrary"),
        name="merge",
    )(attn, conv, wa_all, wc_all, proj, proj)


def _outproj_kernel(a_ref, w_hbm, x_ref, m_ref, o_ref, wf_ref, wb_ref, sem, *, layer, gate_row):
    _prefetched_weight_tile(w_hbm, wf_ref, wb_ref, sem, layer=layer, tile0=0)
    gate = 1.0 + m_ref[0, gate_row:gate_row + 1, :]
    for r0 in range(0, a_ref.shape[0], DOT_ROW_CHUNK):
        rows = slice(r0, r0 + DOT_ROW_CHUNK)
        y = jnp.dot(a_ref[rows, :], wb_ref[...], preferred_element_type=F32)
        o_ref[rows, :] = x_ref[rows, :] + gate * y


def _outproj(a, w_all, layer, x, mod_l, gate_row):
    m, k = a.shape
    n = w_all.shape[2]
    tm, tn = MERGE_TM, MERGE_TN
    return pl.pallas_call(
        functools.partial(_outproj_kernel, layer=layer, gate_row=gate_row),
        grid=(n // tn, m // tm),
        in_specs=[
            pl.BlockSpec((tm, k), lambda j, i: (i, 0)),
            pl.BlockSpec(memory_space=pl.ANY),
            pl.BlockSpec((tm, tn), lambda j, i: (i, j)),
            pl.BlockSpec((1, N_MOD, tn), lambda j, i: (i * tm // SEQ, 0, j)),
        ],
        out_specs=pl.BlockSpec((tm, tn), lambda j, i: (i, j)),
        out_shape=jax.ShapeDtypeStruct((m, n), F32),
        scratch_shapes=_weight_stream_scratch(k, tn),
        compiler_params=_params("arbitrary", "arbitrary"),
        name="outproj",
    )(a, w_all, x, mod_l)


def _mlp_in_kernel(h_ref, w_hbm, w2_ref, o_ref, w2b_ref, wf_ref, wb_ref, sem, *, layer):
    _prefetched_weight_tile(w_hbm, wf_ref, wb_ref, sem, layer=layer, tile0=0)
    w2b_ref[...] = w2_ref[0].astype(BF16)
    y = jnp.maximum(jnp.dot(h_ref[...], wb_ref[...], preferred_element_type=F32), 0.0)
    o_ref[...] = (y * y).astype(BF16)


def _mlp_in(h, w_all, w2_all, layer):
    m, d = h.shape
    n = w_all.shape[2]
    tm, tn = INPROJ_TM, INPROJ_TN
    nm = m // tm
    slab = w2_all.shape[1] // ((n // tn) * nm)
    assert slab * (n // tn) * nm == w2_all.shape[1] and slab % 16 == 0
    return pl.pallas_call(
        functools.partial(_mlp_in_kernel, layer=layer),
        grid=(n // tn, nm),
        in_specs=[
            pl.BlockSpec((tm, d), lambda j, i: (i, 0)),
            pl.BlockSpec(memory_space=pl.ANY),
            pl.BlockSpec((1, slab, w2_all.shape[2]), lambda j, i: (layer, j * nm + i, 0)),
        ],
        out_specs=[
            pl.BlockSpec((tm, tn), lambda j, i: (i, j)),
            pl.BlockSpec((slab, w2_all.shape[2]), lambda j, i: (j * nm + i, 0)),
        ],
        out_shape=[jax.ShapeDtypeStruct((m, n), BF16),
                   jax.ShapeDtypeStruct(w2_all.shape[1:], BF16)],
        scratch_shapes=_weight_stream_scratch(d, tn),
        compiler_params=_params("arbitrary", "arbitrary"),
        name="mlp_in",
    )(h, w_all, w2_all)


def _mlp_out_kernel(a_ref, w_ref, x_ref, m_ref, o_ref, *, gate_row):
    k = pl.program_id(2)
    last = pl.num_programs(2) - 1

    def partial_product():
        return jnp.dot(a_ref[...], w_ref[...], preferred_element_type=F32)

    @pl.when(k == 0)
    def _():
        o_ref[...] = partial_product()

    @pl.when((k > 0) & (k < last))
    def _():
        o_ref[...] += partial_product()

    @pl.when(k == last)
    def _():
        y = o_ref[...] + partial_product()
        o_ref[...] = x_ref[...] + (1.0 + m_ref[0, gate_row:gate_row + 1, :]) * y


def _mlp_out(a, w, x, mod_l, gate_row):
    m, kdim = a.shape
    n = w.shape[1]
    tm, tn, tk = 1024, 1024, 4096
    return pl.pallas_call(
        functools.partial(_mlp_out_kernel, gate_row=gate_row),
        grid=(m // tm, n // tn, kdim // tk),
        in_specs=[
            pl.BlockSpec((tm, tk), lambda i, j, k: (i, k)),
            pl.BlockSpec((tk, tn), lambda i, j, k: (k, j)),
            pl.BlockSpec((tm, tn), lambda i, j, k: (i, j)),
            pl.BlockSpec((1, N_MOD, tn), lambda i, j, k: (i * tm // SEQ, 0, j)),
        ],
        out_specs=pl.BlockSpec((tm, tn), lambda i, j, k: (i, j)),
        out_shape=jax.ShapeDtypeStruct((m, n), F32),
        compiler_params=_params("arbitrary", "arbitrary", "arbitrary"),
        name="mlp_out",
    )(a, w, x, mod_l)


def _rope_tables(seq):
    rows = seq // GRID_W
    row = jnp.repeat(jnp.arange(rows, dtype=F32), GRID_W)
    col = jnp.tile(jnp.arange(GRID_W, dtype=F32), rows)
    half = HEAD_DIM // 2
    freqs = ROPE_THETA ** (-jnp.arange(0, half, 2, dtype=F32) / half)
    ang_r = row[:, None] * freqs
    ang_c = col[:, None] * freqs
    zeros = jnp.zeros_like(ang_r)
    cos = jnp.concatenate([jnp.cos(ang_r), jnp.cos(ang_r), jnp.cos(ang_c), jnp.cos(ang_c)], axis=1)
    sin_lo = jnp.concatenate([-jnp.sin(ang_r), zeros, -jnp.sin(ang_c), zeros], axis=1)
    sin_hi = jnp.concatenate([zeros, jnp.sin(ang_r), zeros, jnp.sin(ang_c)], axis=1)
    return cos, sin_lo, sin_hi


def kernel(x, c, mod_w, mod_b, norm1_g, norm2_g, w_in, q_norm_g, k_norm_g, conv_w,
           w_branch_attn, w_branch_conv, w_out, w_mlp_in, w_mlp_out, final_norm_g):
    batch, seq, d = x.shape
    depth = mod_w.shape[0]
    assert (seq, d) == (SEQ, D_MODEL) and batch <= MOD_ROWS and w_in.shape[2] == IN_WIDTH
    m = batch * seq

    c_pad = jnp.zeros((MOD_ROWS, d), F32).at[:batch].set(c)
    mod = _modulation(c_pad, mod_w, mod_b)
    mod = mod[:, :batch].reshape(depth, batch, N_MOD, d)
    cos, sin_lo, sin_hi = _rope_tables(seq)

    xf = x.reshape(m, d)
    for l in range(depth):
        mod_l = mod[l]
        h = _norm_mod(xf, norm1_g[l], mod_l, sh_row=0, sc_row=1)
        tables = (cos, sin_lo, sin_hi)
        q = _inproj(h, w_in, l, 0, ATTN_WIDTH, q_norm_g[l], tables,
                    n_rope_heads=INPROJ_TN // HEAD_DIM, gain_scale=HEAD_DIM ** -0.5 * LOG2_E)
        kv = _inproj(h, w_in, l, COL_KV, 2 * KV_WIDTH, k_norm_g[l], tables, n_rope_heads=N_KV_HEADS)
        proj = _inproj(h, w_in, l, COL_REST, REST_WIDTH)
        attn = _attention(q, kv, batch)
        conv = _gated_conv(proj, conv_w, l)
        merged = _merge(attn, conv, w_branch_attn, w_branch_conv, l, proj)
        xf = _outproj(merged, w_out, l, xf, mod_l, gate_row=2)
        h = _norm_mod(xf, norm2_g[l], mod_l, sh_row=3, sc_row=4)
        a, w2b = _mlp_in(h, w_mlp_in, w_mlp_out, l)
        xf = _mlp_out(a, w2b, xf, mod_l, gate_row=5)
    return _final_norm(xf, final_norm_g).reshape(batch, seq, d)
```
